```python
import math
import jax, jax.numpy as jnp
from jax import lax
import numpy as np

D_MODEL = 1024
BATCH = 2
SEQ = 8192
DEPTH = 4
DEC_BATCH = 8
DEC_SEQ = 2048
PAST_LEN = 128

M_HEADS = 4
M_HEAD_DIM = 256
M_WIDTH = M_HEADS * M_HEAD_DIM
M_CHUNK = 128
CONV_W = 5
A_HEADS = 8
A_HEAD_DIM = 64
A_QK_WIDTH = 2 * A_HEADS * A_HEAD_DIM
A_WIDTH = A_HEADS * 2 * A_HEAD_DIM
Q_BLOCK = 128
ROPE_THETA = 10000.0
D_FF = 2752
N_EXPERTS = 8
TOP_K = 2
N_DENSE = (DEPTH + 1) // 2
N_MOE = DEPTH // 2
EPS = 1e-6
N_IN = 4 * M_WIDTH + 4 * M_HEADS + 2 * A_QK_WIDTH + A_WIDTH + 2 * D_MODEL

kernel_name = "hybrid_mlstm_diffattn_encoder"


def _split_points():
    sizes = [M_WIDTH, M_WIDTH, M_WIDTH, M_WIDTH, 2 * M_HEADS, 2 * M_HEADS,
             A_QK_WIDTH, A_QK_WIDTH, A_WIDTH, D_MODEL, D_MODEL]
    return [int(s) for s in np.cumsum(sizes)[:-1]]


def rmsnorm(x, g):
    xf = x.astype(jnp.float32)
    y = xf * lax.rsqrt(jnp.mean(xf * xf, axis=-1, keepdims=True) + EPS)
    return (y * g.astype(jnp.float32)).astype(x.dtype)


def rope(x):
    S, d = x.shape[1], x.shape[-1]
    inv = 1.0 / (ROPE_THETA ** (jnp.arange(0, d, 2, dtype=jnp.float32) / d))
    ang = jnp.arange(S, dtype=jnp.float32)[:, None] * inv[None, :]
    cos = jnp.cos(ang)[None, :, None, :]
    sin = jnp.sin(ang)[None, :, None, :]
    xf = x.astype(jnp.float32)
    x1, x2 = jnp.split(xf, 2, axis=-1)
    out = jnp.concatenate([x1 * cos - x2 * sin, x2 * cos + x1 * sin], axis=-1)
    return out.astype(x.dtype)


def centred_depthwise_conv(x, w):
    C = x.shape[-1]
    pad = CONV_W // 2
    return lax.conv_general_dilated(
        x, w.astype(x.dtype)[:, None, :], window_strides=(1,), padding=((pad, pad),),
        dimension_numbers=('NWC', 'WIO', 'NWC'), feature_group_count=C)


def mlstm_chunkwise(q, k, v, log_i, log_f):
    B, H, S, dh = q.shape
    nc = S // M_CHUNK
    chunk = lambda t: jnp.moveaxis(t.reshape(B, H, nc, M_CHUNK, *t.shape[3:]), 2, 0)
    xs = (chunk(q), chunk(k), chunk(v), chunk(log_i), chunk(log_f))
    causal_in_chunk = jnp.tril(jnp.ones((M_CHUNK, M_CHUNK), dtype=bool))

    def step(carry, inp):
        C, n, m = carry
        qc, kc, vc, ic, fc = inp
        b = jnp.cumsum(fc, axis=-1)
        logd = b[..., :, None] - b[..., None, :] + ic[..., None, :]
        logd = jnp.where(causal_in_chunk, logd, -jnp.inf)
        m_inter = b + m[..., None]
        m_t = jnp.maximum(m_inter, jnp.max(logd, axis=-1))
        w_inter = jnp.exp(m_inter - m_t)
        s = jnp.einsum('bhtd,bhsd->bhts', qc, kc) * jnp.exp(logd - m_t[..., None])
        num = (w_inter[..., None] * jnp.einsum('bhtd,bhde->bhte', qc, C)
               + jnp.einsum('bhts,bhse->bhte', s, vc))
        den = w_inter * jnp.einsum('bhtd,bhd->bht', qc, n) + jnp.sum(s, axis=-1)
        h = num / jnp.maximum(jnp.abs(den), jnp.exp(-m_t))[..., None]
        g = b[..., -1]
        r = g[..., None] - b + ic
        m_new = jnp.maximum(g + m, jnp.max(r, axis=-1))
        decay = jnp.exp(g + m - m_new)
        ws = jnp.exp(r - m_new[..., None])
        C_new = decay[..., None, None] * C + jnp.einsum('bhs,bhsd,bhse->bhde', ws, kc, vc)
        n_new = decay[..., None] * n + jnp.einsum('bhs,bhsd->bhd', ws, kc)
        return (C_new, n_new, m_new), h

    init = (jnp.zeros((B, H, dh, dh), jnp.float32),
            jnp.zeros((B, H, dh), jnp.float32),
            jnp.zeros((B, H), jnp.float32))
    _, hs = lax.scan(step, init, xs)
    return jnp.moveaxis(hs, 0, 2).reshape(B, H, S, dh)


def diff_attention(q, k, v, lam):
    B, S = q.shape[0], q.shape[1]
    nq = S // Q_BLOCK
    scale = A_HEAD_DIM ** -0.5
    qb = q.reshape(B, nq, Q_BLOCK, 2 * A_HEADS, A_HEAD_DIM).transpose(1, 0, 2, 3, 4)

    def block(qblk):
        s = jnp.einsum('bqhd,bkhd->bhqk', qblk, k).astype(jnp.float32) * scale
        p = jax.nn.softmax(s, axis=-1).reshape(B, A_HEADS, 2, Q_BLOCK, S)
        a = (p[:, :, 0] - lam * p[:, :, 1]).astype(v.dtype)
        return jnp.einsum('bhqk,bkhe->bqhe', a, v)

    o = lax.map(block, qb)
    return o.transpose(1, 0, 2, 3, 4).reshape(B, S, A_HEADS, 2 * A_HEAD_DIM)


def token_mixer(h, lam_init, w_in, conv_qk, b_igate, b_fgate, m_norm_g,
                lam_q1, lam_k1, lam_q2, lam_k2, a_norm_g, w_br_m, w_br_a, w_out):
    B, S, _ = h.shape
    proj = h @ w_in
    mq, mk, mv, mo, mi, mf, aq, ak, av, gm, ga = jnp.split(proj, _split_points(), axis=-1)

    qk = jax.nn.silu(centred_depthwise_conv(jnp.concatenate([mq, mk], axis=-1), conv_qk))
    mq, mk = jnp.split(qk, 2, axis=-1)
    to_heads = lambda t: t.reshape(B, S, M_HEADS, M_HEAD_DIM).transpose(0, 2, 1, 3).astype(jnp.float32)
    q = to_heads(mq)
    k = to_heads(mk) * (M_HEAD_DIM ** -0.5)
    v = to_heads(mv)
    gates = lambda t, bias: (t.astype(jnp.float32).reshape(B, S, 2, M_HEADS)
                             + bias.astype(jnp.float32)).transpose(2, 0, 3, 1)
    log_i = gates(mi, b_igate)
    log_f = jax.nn.log_sigmoid(gates(mf, b_fgate))
    h_fwd = mlstm_chunkwise(q, k, v, log_i[0], log_f[0])
    rev = lambda t: jnp.flip(t, axis=2)
    h_bwd = rev(mlstm_chunkwise(rev(q), rev(k), rev(v),
                                jnp.flip(log_i[1], axis=-1), jnp.flip(log_f[1], axis=-1)))
    hm = rmsnorm(h_fwd + h_bwd, m_norm_g.reshape(M_HEADS, 1, M_HEAD_DIM))
    hm = hm.transpose(0, 2, 1, 3).reshape(B, S, M_WIDTH).astype(h.dtype) * jax.nn.sigmoid(mo)

    qa = rope(aq.reshape(B, S, 2 * A_HEADS, A_HEAD_DIM))
    ka = rope(ak.reshape(B, S, 2 * A_HEADS, A_HEAD_DIM))
    va = av.reshape(B, S, A_HEADS, 2 * A_HEAD_DIM)
    f32 = lambda t: t.astype(jnp.float32)
    lam = (jnp.exp(jnp.sum(f32(lam_q1) * f32(lam_k1)))
           - jnp.exp(jnp.sum(f32(lam_q2) * f32(lam_k2))) + lam_init)
    ha = diff_attention(qa, ka, va, lam)
    ha = (rmsnorm(ha, a_norm_g) * (1.0 - lam_init)).reshape(B, S, A_WIDTH)

    merged = jax.nn.sigmoid(gm) * (hm @ w_br_m) + jax.nn.sigmoid(ga) * (ha @ w_br_a)
    return merged @ w_out


def swiglu(x, w_gu, w_down):
    g, u = jnp.split(x @ w_gu, 2, axis=-1)
    return (jax.nn.silu(g) * u) @ w_down


def moe_swiglu(x, router, w_gu, w_down):
    logits = (x @ router).astype(jnp.float32)
    top_v, top_i = lax.top_k(logits, TOP_K)
    top_p = jax.nn.softmax(top_v, axis=-1)
    gate = jnp.einsum('bske,bsk->bse', jax.nn.one_hot(top_i, N_EXPERTS, dtype=jnp.float32), top_p)
    y = jnp.zeros_like(x)
    for e in range(N_EXPERTS):
        y = y + gate[..., e:e + 1].astype(x.dtype) * swiglu(x, w_gu[e], w_down[e])
    return y


def trunk(x, norm_mix_g, w_in, conv_qk, b_igate, b_fgate, m_norm_g, lam_q1, lam_k1,
          lam_q2, lam_k2, a_norm_g, w_br_m, w_br_a, w_out, norm_ffn_g, dense_w_gu,
          dense_w_down, moe_router, moe_w_gu, moe_w_down, final_norm_g):
    for l in range(DEPTH):
        lam_init = 0.8 - 0.6 * math.exp(-0.3 * l)
        h = rmsnorm(x, norm_mix_g[l])
        x = x + token_mixer(h, lam_init, w_in[l], conv_qk[l], b_igate[l], b_fgate[l], m_norm_g[l],
                            lam_q1[l], lam_k1[l], lam_q2[l], lam_k2[l], a_norm_g[l],
                            w_br_m[l], w_br_a[l], w_out[l])
        h = rmsnorm(x, norm_ffn_g[l])
        if l % 2 == 0:
            x = x + swiglu(h, dense_w_gu[l // 2], dense_w_down[l // 2])
        else:
            x = x + moe_swiglu(h, moe_router[l // 2], moe_w_gu[l // 2], moe_w_down[l // 2])
    return rmsnorm(x, final_norm_g)


def setup_inputs(seed: int = 0) -> dict:
    key = jax.random.key(seed)
    ks = jax.random.split(key, 24)
    nrm = lambda k, shape, scale: jax.random.normal(k, shape, jnp.float32) * scale
    return {
        "x_prompt": nrm(ks[0], (BATCH, SEQ, D_MODEL), 1.0),
        "x_sample": nrm(ks[1], (DEC_BATCH, DEC_SEQ, D_MODEL), 1.0),
        "norm_mix_g": 1.0 + nrm(ks[2], (DEPTH, D_MODEL), 0.02),
        "w_in": nrm(ks[3], (DEPTH, D_MODEL, N_IN), D_MODEL ** -0.5),
        "conv_qk": nrm(ks[4], (DEPTH, CONV_W, 2 * M_WIDTH), CONV_W ** -0.5),
        "b_igate": nrm(ks[5], (DEPTH, 2, M_HEADS), 0.1),
        "b_fgate": jnp.linspace(3.0, 6.0, M_HEADS, dtype=jnp.float32)[None, None, :]
                    + nrm(ks[6], (DEPTH, 2, M_HEADS), 0.1),
        "m_norm_g": 1.0 + nrm(ks[7], (DEPTH, M_WIDTH), 0.02),
        "lam_q1": nrm(ks[8], (DEPTH, A_HEAD_DIM), 0.1),
        "lam_k1": nrm(ks[9], (DEPTH, A_HEAD_DIM), 0.1),
        "lam_q2": nrm(ks[10], (DEPTH, A_HEAD_DIM), 0.1),
        "lam_k2": nrm(ks[11], (DEPTH, A_HEAD_DIM), 0.1),
        "a_norm_g": 1.0 + nrm(ks[12], (DEPTH, 2 * A_HEAD_DIM), 0.02),
        "w_br_m": nrm(ks[13], (DEPTH, M_WIDTH, D_MODEL), M_WIDTH ** -0.5),
        "w_br_a": nrm(ks[14], (DEPTH, A_WIDTH, D_MODEL), A_WIDTH ** -0.5),
        "w_out": nrm(ks[15], (DEPTH, D_MODEL, D_MODEL), D_MODEL ** -0.5),
        "norm_ffn_g": 1.0 + nrm(ks[16], (DEPTH, D_MODEL), 0.02),
        "dense_w_gu": nrm(ks[17], (N_DENSE, D_MODEL, 2 * D_FF), D_MODEL ** -0.5),
        "dense_w_down": nrm(ks[18], (N_DENSE, D_FF, D_MODEL), D_FF ** -0.5),
        "moe_router": nrm(ks[19], (N_MOE, D_MODEL, N_EXPERTS), D_MODEL ** -0.5),
        "moe_w_gu": nrm(ks[20], (N_MOE, N_EXPERTS, D_MODEL, 2 * D_FF), D_MODEL ** -0.5),
        "moe_w_down": nrm(ks[21], (N_MOE, N_EXPERTS, D_FF, D_MODEL), D_FF ** -0.5),
        "final_norm_g": 1.0 + nrm(ks[22], (D_MODEL,), 0.02),
    }


def reference(x_prompt, x_sample, norm_mix_g, w_in, conv_qk, b_igate, b_fgate, m_norm_g,
              lam_q1, lam_k1, lam_q2, lam_k2, a_norm_g, w_br_m, w_br_a, w_out, norm_ffn_g,
              dense_w_gu, dense_w_down, moe_router, moe_w_gu, moe_w_down, final_norm_g):
    weights = (norm_mix_g, w_in, conv_qk, b_igate, b_fgate, m_norm_g, lam_q1, lam_k1,
               lam_q2, lam_k2, a_norm_g, w_br_m, w_br_a, w_out, norm_ffn_g, dense_w_gu,
               dense_w_down, moe_router, moe_w_gu, moe_w_down, final_norm_g)
    y_prompt = trunk(x_prompt, *weights)
    y_sample = trunk(x_sample, *weights)
    return (y_prompt, y_sample)
```

```python
import functools
import math

import jax
import jax.numpy as jnp
from jax import lax
from jax.experimental import pallas as pl
from jax.experimental.pallas import tpu as pltpu

F32 = jnp.float32
BF16 = jnp.bfloat16

D_MODEL = 1024
M_HEADS = 4
M_HEAD_DIM = 256
M_CHUNK = 128
CONV_W = 5
A_HEADS = 8
A_HEAD_DIM = 64
A_VDIM = 2 * A_HEAD_DIM
D_FF = 2752
N_EXPERTS = 8
EPS = 1e-6
ROPE_THETA = 10000.0

LANES = 128
F_PAD = 2816
N_MAIN_BLOCKS = 9
BLK_MQ, BLK_MK, BLK_MV, BLK_MO, BLK_AQ, BLK_AK, BLK_AV, BLK_GM, BLK_GA = range(9)
VMEM_LIMIT = 56 * 1024 * 1024


def _cparams(sem):
    return pltpu.CompilerParams(dimension_semantics=sem, vmem_limit_bytes=VMEM_LIMIT)


def _rms(x, g):
    return x * lax.rsqrt(jnp.mean(x * x, axis=-1, keepdims=True) + EPS) * g


def _sigmoid(x):
    return 1.0 / (1.0 + jnp.exp(-x))


def _seq_pos(row0, tp, sp, ss):
    in_prompt = row0 < tp
    pos = jnp.where(in_prompt, row0 % sp, (row0 - tp) % ss)
    slen = jnp.where(in_prompt, sp, ss)
    return pos, slen


def _inproj_kernel(x_ref, g_ref, w_ref, wgate_ref, bgate_ref, cos_ref, sin_ref,
                   out_ref, gates_ref, h_scr):
    j = pl.program_id(1)

    @pl.when(j == 0)
    def _():
        hb = _rms(x_ref[...], g_ref[...]).astype(BF16)
        h_scr[...] = hb
        gt = jnp.dot(hb, wgate_ref[...], preferred_element_type=F32) + bgate_ref[...]
        lane = lax.broadcasted_iota(jnp.int32, gt.shape, 1)
        log_sig = jnp.minimum(gt, 0.0) - jnp.log1p(jnp.exp(-jnp.abs(gt)))
        is_fgate = (lane >= 2 * M_HEADS) & (lane < 4 * M_HEADS)
        gates_ref[...] = jnp.where(is_fgate, log_sig, gt)

    acc = jnp.dot(h_scr[...], w_ref[...], preferred_element_type=F32)
    is_sig = (j == BLK_MO) | (j == BLK_GM) | (j == BLK_GA)
    is_rope = (j == BLK_AQ) | (j == BLK_AK)

    @pl.when(is_sig)
    def _():
        out_ref[...] = _sigmoid(acc).astype(out_ref.dtype)

    @pl.when(is_rope)
    def _():
        n = acc.shape[1]
        half = A_HEAD_DIM // 2
        lane = lax.broadcasted_iota(jnp.int32, acc.shape, 1)
        lower = (lane % A_HEAD_DIM) < half
        partner = jnp.where(lower, pltpu.roll(acc, n - half, axis=1),
                            pltpu.roll(acc, half, axis=1))
        reps = n // LANES
        cos = jnp.tile(cos_ref[...], (1, reps))
        sin = jnp.tile(sin_ref[...], (1, reps))
        roped = acc * cos + partner * sin
        scale = jnp.where(j == BLK_AQ, A_HEAD_DIM ** -0.5, 1.0)
        out_ref[...] = (roped * scale).astype(out_ref.dtype)

    @pl.when(jnp.logical_not(is_sig | is_rope))
    def _():
        out_ref[...] = acc.astype(out_ref.dtype)


def _inproj(x, g, w_main, w_gate, b_gate, cos_t, sin_t, geom):
    T = x.shape[0]
    tp, sp, ss = geom
    tm = min(1024, ss)

    def pos_block(i, j):
        pos, _ = _seq_pos(i * tm, tp, sp, ss)
        return (pos // tm, 0)

    return pl.pallas_call(
        _inproj_kernel,
        grid=(T // tm, N_MAIN_BLOCKS),
        in_specs=[
            pl.BlockSpec((tm, D_MODEL), lambda i, j: (i, 0)),
            pl.BlockSpec((1, D_MODEL), lambda i, j: (0, 0)),
            pl.BlockSpec((D_MODEL, D_MODEL), lambda i, j: (0, j)),
            pl.BlockSpec((D_MODEL, LANES), lambda i, j: (0, 0)),
            pl.BlockSpec((1, LANES), lambda i, j: (0, 0)),
            pl.BlockSpec((tm, LANES), pos_block),
            pl.BlockSpec((tm, LANES), pos_block),
        ],
        out_specs=[
            pl.BlockSpec((tm, D_MODEL), lambda i, j: (i, j)),
            pl.BlockSpec((tm, LANES), lambda i, j: (i, 0)),
        ],
        out_shape=[
            jax.ShapeDtypeStruct((T, N_MAIN_BLOCKS * D_MODEL), BF16),
            jax.ShapeDtypeStruct((T, LANES), F32),
        ],
        scratch_shapes=[pltpu.VMEM((tm, D_MODEL), BF16)],
        compiler_params=_cparams(("parallel", "arbitrary")),
        name="inproj",
    )(x, g, w_main, w_gate, b_gate, cos_t, sin_t)


HALO = 16


def _conv_kernel(prev_ref, main_ref, next_ref, w_ref, out_ref, ext_scr, *, geom, rows):
    tp, sp, ss = geom
    i = pl.program_id(0)
    j = pl.program_id(1)
    pos, slen = _seq_pos(i * rows, tp, sp, ss)
    prev_ok = jnp.where(pos == 0, 0.0, 1.0)
    next_ok = jnp.where(pos + rows == slen, 0.0, 1.0)
    ext_scr[0:HALO, :] = prev_ref[...].astype(F32) * prev_ok
    ext_scr[HALO:HALO + rows, :] = main_ref[...].astype(F32)
    ext_scr[HALO + rows:, :] = next_ref[...].astype(F32) * next_ok
    pad = CONV_W // 2
    acc = None
    for t in range(CONV_W):
        term = ext_scr[HALO - pad + t:HALO - pad + t + rows, :] * w_ref[t:t + 1, :]
        acc = term if acc is None else acc + term
    y = acc * _sigmoid(acc)
    kscale = jnp.where(j == 1, M_HEAD_DIM ** -0.5, 1.0)
    out_ref[...] = (y * kscale).astype(out_ref.dtype)


def _conv_silu(proj, conv_w, geom):
    T = proj.shape[0]
    tp, sp, ss = geom
    rows = min(512, ss)
    hb = rows // HALO
    nh = T // HALO
    return pl.pallas_call(
        functools.partial(_conv_kernel, geom=geom, rows=rows),
        grid=(T // rows, 2),
        in_specs=[
            pl.BlockSpec((HALO, D_MODEL), lambda i, j: (jnp.maximum(i * hb - 1, 0), j)),
            pl.BlockSpec((rows, D_MODEL), lambda i, j: (i, j)),
            pl.BlockSpec((HALO, D_MODEL), lambda i, j: (jnp.minimum((i + 1) * hb, nh - 1), j)),
            pl.BlockSpec((CONV_W, D_MODEL), lambda i, j: (0, j)),
        ],
        out_specs=pl.BlockSpec((rows, D_MODEL), lambda i, j: (i, j)),
        out_shape=jax.ShapeDtypeStruct((T, 2 * D_MODEL), BF16),
        scratch_shapes=[pltpu.VMEM((rows + 2 * HALO, D_MODEL), F32)],
        compiler_params=_cparams(("parallel", "parallel")),
        name="conv_silu",
    )(proj, proj, proj, conv_w)


def _mlstm_kernel(*refs, reverse, geom):
    tp, sp, ss = geom
    if reverse:
        (q_ref, k_ref, v_ref, gates_ref, hf_ref, mo_ref, ng_ref,
         out_ref, c_scr, n_scr, m_scr) = refs
    else:
        q_ref, k_ref, v_ref, gates_ref, out_ref, c_scr, n_scr, m_scr = refs
    L = M_CHUNK
    dh = M_HEAD_DIM
    c = pl.program_id(0)
    chunk = (pl.num_programs(0) - 1 - c) if reverse else c
    pos, slen = _seq_pos(chunk * L, tp, sp, ss)
    first = (pos == slen - L) if reverse else (pos == 0)

    @pl.when(first)
    def _():
        c_scr[...] = jnp.zeros_like(c_scr)
        n_scr[...] = jnp.zeros_like(n_scr)
        m_scr[...] = jnp.zeros_like(m_scr)

    gates = gates_ref[...]
    row = lax.broadcasted_iota(jnp.int32, (L, L), 0)
    col = lax.broadcasted_iota(jnp.int32, (L, L), 1)
    valid = (col >= row) if reverse else (col <= row)
    tri = jnp.where(valid, 1.0, 0.0).astype(BF16)
    g1 = gates.astype(BF16)
    r1 = gates - g1.astype(F32)
    g2 = r1.astype(BF16)
    g3 = (r1 - g2.astype(F32)).astype(BF16)
    cum = (jnp.dot(tri, g1, preferred_element_type=F32)
           + jnp.dot(tri, g2, preferred_element_type=F32)
           + jnp.dot(tri, g3, preferred_element_type=F32))
    cum_t = cum.T
    gates_t = gates.T
    last = 0 if reverse else L - 1

    for h in range(M_HEADS):
        ci = (M_HEADS if reverse else 0) + h
        cf = 2 * M_HEADS + ci
        hs = slice(h * dh, (h + 1) * dh)
        b_col = cum[:, cf:cf + 1]
        b_row = cum_t[cf:cf + 1, :]
        i_col = gates[:, ci:ci + 1]
        i_row = gates_t[ci:ci + 1, :]
        m_prev = m_scr[h, 0:1, 0:1]
        n_prev = n_scr[h]
        c_prev = c_scr[h]
        q = q_ref[:, hs]
        k = k_ref[:, hs]
        v = v_ref[:, hs]

        logd = jnp.where(valid, b_col - b_row + i_row, -jnp.inf)
        m_inter = b_col + m_prev
        m_t = jnp.maximum(m_inter, jnp.max(logd, axis=-1, keepdims=True))
        w_inter = jnp.exp(m_inter - m_t)
        s = lax.dot_general(q, k, (((1,), (1,)), ((), ())),
                            preferred_element_type=F32) * jnp.exp(logd - m_t)
        num = (w_inter * jnp.dot(q, c_prev.astype(BF16), preferred_element_type=F32)
               + jnp.dot(s.astype(BF16), v, preferred_element_type=F32))
        den = (w_inter * jnp.sum(q.astype(F32) * n_prev, axis=-1, keepdims=True)
               + jnp.sum(s, axis=-1, keepdims=True))
        hout = num / jnp.maximum(jnp.abs(den), jnp.exp(-m_t))

        g_tot = cum[last:last + 1, cf:cf + 1]
        r_col = g_tot - b_col + i_col
        r_row = g_tot - b_row + i_row
        m_new = jnp.maximum(g_tot + m_prev, jnp.max(r_row, axis=-1, keepdims=True))
        decay = jnp.exp(g_tot + m_prev - m_new)
        kw = k.astype(F32) * jnp.exp(r_col - m_new)
        c_scr[h] = decay * c_prev + jnp.dot(kw.T.astype(BF16), v,
                                            preferred_element_type=F32)
        n_scr[h] = decay * n_prev + jnp.sum(kw, axis=0, keepdims=True)
        m_scr[h] = jnp.broadcast_to(m_new, m_scr.shape[1:])

        if reverse:
            hsum = hout + hf_ref[:, hs]
            normed = _rms(hsum, ng_ref[:, hs])
            out_ref[:, hs] = (normed * mo_ref[:, hs].astype(F32)).astype(out_ref.dtype)
        else:
            out_ref[:, hs] = hout


def _mlstm(qk, proj, gates, geom, h_fwd=None, m_norm_g=None):
    T = qk.shape[0]
    reverse = h_fwd is not None
    nc = T // M_CHUNK
    cm = (lambda c: nc - 1 - c) if reverse else (lambda c: c)
    in_specs = [
        pl.BlockSpec((M_CHUNK, D_MODEL), lambda c: (cm(c), 0)),
        pl.BlockSpec((M_CHUNK, D_MODEL), lambda c: (cm(c), 1)),
        pl.BlockSpec((M_CHUNK, D_MODEL), lambda c: (cm(c), BLK_MV)),
        pl.BlockSpec((M_CHUNK, LANES), lambda c: (cm(c), 0)),
    ]
    args = [qk, qk, proj, gates]
    if reverse:
        in_specs += [
            pl.BlockSpec((M_CHUNK, D_MODEL), lambda c: (cm(c), 0)),
            pl.BlockSpec((M_CHUNK, D_MODEL), lambda c: (cm(c), BLK_MO)),
            pl.BlockSpec((1, D_MODEL), lambda c: (0, 0)),
        ]
        args += [h_fwd, proj, m_norm_g]
    return pl.pallas_call(
        functools.partial(_mlstm_kernel, reverse=reverse, geom=geom),
        grid=(nc,),
        in_specs=in_specs,
        out_specs=pl.BlockSpec((M_CHUNK, D_MODEL), lambda c: (cm(c), 0)),
        out_shape=jax.ShapeDtypeStruct((T, D_MODEL), BF16 if reverse else F32),
        scratch_shapes=[
            pltpu.VMEM((M_HEADS, M_HEAD_DIM, M_HEAD_DIM), F32),
            pltpu.VMEM((M_HEADS, 1, M_HEAD_DIM), F32),
            pltpu.VMEM((M_HEADS, 8, LANES), F32),
        ],
        compiler_params=_cparams(("arbitrary",)),
        name="mlstm_bwd" if reverse else "mlstm_fwd",
    )(*args)


def _attn_kernel(q_ref, k_ref, v_ref, lam_ref, ng_ref, out_ref, *, geom, tq, tk, kb,
                 lam_init):
    tp, sp, ss = geom
    r0 = pl.program_id(1) * tq
    pos, slen = _seq_pos(r0, tp, sp, ss)
    off = (r0 - pos) % kb
    nk = slen // tk
    d = A_HEAD_DIM
    q = q_ref[...]
    q1 = q[:, :d]
    q2 = q[:, d:]

    def stream(qs, ks, vs, m, l, a):
        s = lax.dot_general(qs, ks, (((1,), (1,)), ((), ())), preferred_element_type=F32)
        m_new = jnp.maximum(m, jnp.max(s, axis=-1, keepdims=True))
        alpha = jnp.exp(m - m_new)
        p = jnp.exp(s - m_new)
        l = alpha * l + jnp.sum(p, axis=-1, keepdims=True)
        a = alpha * a + jnp.dot(p.astype(BF16), vs, preferred_element_type=F32)
        return m_new, l, a

    def body(t, carry):
        m1, l1, a1, m2, l2, a2 = carry
        start = pl.multiple_of(off + t * tk, tk)
        kc = k_ref[pl.ds(start, tk), :]
        vc = v_ref[pl.ds(start, tk), :]
        m1, l1, a1 = stream(q1, kc[:, :d], vc, m1, l1, a1)
        m2, l2, a2 = stream(q2, kc[:, d:], vc, m2, l2, a2)
        return m1, l1, a1, m2, l2, a2

    neg = jnp.full((tq, 1), -jnp.inf, F32)
    zero1 = jnp.zeros((tq, 1), F32)
    zeroa = jnp.zeros((tq, A_VDIM), F32)
    _, l1, a1, _, l2, a2 = lax.fori_loop(0, nk, body, (neg, zero1, zeroa, neg, zero1, zeroa))

    lv = lam_ref[...]
    lam = (jnp.exp(jnp.sum(lv[0:1, :] * lv[1:2, :], axis=-1, keepdims=True))
           - jnp.exp(jnp.sum(lv[2:3, :] * lv[3:4, :], axis=-1, keepdims=True)) + lam_init)
    o = a1 / l1 - lam * (a2 / l2)
    out_ref[...] = (_rms(o, ng_ref[...]) * (1.0 - lam_init)).astype(out_ref.dtype)


def _attention(proj, lam_vecs, a_norm_g, geom, lam_init):
    T = proj.shape[0]
    tp, sp, ss = geom
    kb = max(sp, ss)
    tq = min(512, ss)
    tk = min(512, ss)
    hpb = D_MODEL // A_VDIM
    return pl.pallas_call(
        functools.partial(_attn_kernel, geom=geom, tq=tq, tk=tk, kb=kb, lam_init=lam_init),
        grid=(A_HEADS, T // tq),
        in_specs=[
            pl.BlockSpec((tq, A_VDIM), lambda h, r: (r, BLK_AQ * hpb + h)),
            pl.BlockSpec((kb, A_VDIM), lambda h, r: ((r * tq) // kb, BLK_AK * hpb + h)),
            pl.BlockSpec((kb, A_VDIM), lambda h, r: ((r * tq) // kb, BLK_AV * hpb + h)),
            pl.BlockSpec((8, LANES), lambda h, r: (0, 0)),
            pl.BlockSpec((1, A_VDIM), lambda h, r: (0, 0)),
        ],
        out_specs=pl.BlockSpec((tq, A_VDIM), lambda h, r: (r, h)),
        out_shape=jax.ShapeDtypeStruct((T, D_MODEL), BF16),
        compiler_params=_cparams(("parallel", "parallel")),
        name="diff_attn",
    )(proj, proj, proj, lam_vecs, a_norm_g)


def _merge_kernel(x_ref, hm_ref, ha_ref, gm_ref, ga_ref, wm_ref, wa_ref, wo_ref, out_ref):
    bm = jnp.dot(hm_ref[...], wm_ref[...], preferred_element_type=F32)
    ba = jnp.dot(ha_ref[...], wa_ref[...], preferred_element_type=F32)
    merged = gm_ref[...].astype(F32) * bm + ga_ref[...].astype(F32) * ba
    out_ref[...] = x_ref[...] + jnp.dot(merged.astype(BF16), wo_ref[...],
                                        preferred_element_type=F32)


def _merge(x, hm, ha, proj, w_m, w_a, w_o):
    T = x.shape[0]
    tm = 512
    row = lambda i: (i, 0)
    const = lambda i: (0, 0)
    wspec = pl.BlockSpec((D_MODEL, D_MODEL), const)
    return pl.pallas_call(
        _merge_kernel,
        grid=(T // tm,),
        in_specs=[
            pl.BlockSpec((tm, D_MODEL), row),
            pl.BlockSpec((tm, D_MODEL), row),
            pl.BlockSpec((tm, D_MODEL), row),
            pl.BlockSpec((tm, D_MODEL), lambda i: (i, BLK_GM)),
            pl.BlockSpec((tm, D_MODEL), lambda i: (i, BLK_GA)),
            wspec, wspec, wspec,
        ],
        out_specs=pl.BlockSpec((tm, D_MODEL), row),
        out_shape=jax.ShapeDtypeStruct((T, D_MODEL), F32),
        compiler_params=_cparams(("parallel",)),
        name="merge_out",
    )(x, hm, ha, proj, proj, w_m, w_a, w_o)


FFN_SUB = 256


def _ffn_kernel(x_ref, g_ref, wg_ref, wu_ref, wd_ref, out_ref):
    x = x_ref[...]
    hb = _rms(x, g_ref[...]).astype(BF16)
    acc = x
    for s in range(F_PAD // FFN_SUB):
        cs = slice(s * FFN_SUB, (s + 1) * FFN_SUB)
        gate = jnp.dot(hb, wg_ref[:, cs], preferred_element_type=F32)
        up = jnp.dot(hb, wu_ref[:, cs], preferred_element_type=F32)
        hid = (gate * _sigmoid(gate) * up).astype(BF16)
        acc = acc + jnp.dot(hid, wd_ref[cs, :], preferred_element_type=F32)
    out_ref[...] = acc


def _ffn(x, g, wg, wu, wd):
    T = x.shape[0]
    tm = 512
    row = lambda i: (i, 0)
    const = lambda i: (0, 0)
    once = pl.Buffered(1)
    return pl.pallas_call(
        _ffn_kernel,
        grid=(T // tm,),
        in_specs=[
            pl.BlockSpec((tm, D_MODEL), row),
            pl.BlockSpec((1, D_MODEL), const),
            pl.BlockSpec((D_MODEL, F_PAD), const, pipeline_mode=once),
            pl.BlockSpec((D_MODEL, F_PAD), const, pipeline_mode=once),
            pl.BlockSpec((F_PAD, D_MODEL), const, pipeline_mode=once),
        ],
        out_specs=pl.BlockSpec((tm, D_MODEL), row),
        out_shape=jax.ShapeDtypeStruct((T, D_MODEL), F32),
        compiler_params=_cparams(("parallel",)),
        name="ffn_dense",
    )(x, g, wg, wu, wd)


MOE_TM = 1024
MOE_RB = 256
MOE_NF = 2
CUM_W = 256


def _router_kernel(x_ref, g_ref, rt_ref, post_ref, posc_ref, gatet_ref, cnt_ref):
    tm = x_ref.shape[0]
    h = _rms(x_ref[...], g_ref[...])
    rt = rt_ref[...]
    h_hi = h.astype(BF16)
    h_lo = (h - h_hi.astype(F32)).astype(BF16)
    r_hi = rt.astype(BF16)
    r_lo = (rt - r_hi.astype(F32)).astype(BF16)
    nt = (((1,), (1,)), ((), ()))
    logits = (lax.dot_general(r_hi, h_hi, nt, preferred_element_type=F32)
              + lax.dot_general(r_hi, h_lo, nt, preferred_element_type=F32)
              + lax.dot_general(r_lo, h_hi, nt, preferred_element_type=F32))
    eidx = lax.broadcasted_iota(jnp.int32, logits.shape, 0)
    v1 = jnp.max(logits, axis=0, keepdims=True)
    i1 = jnp.min(jnp.where(logits == v1, eidx, N_EXPERTS), axis=0, keepdims=True)
    sel1 = eidx == i1
    rest = jnp.where(sel1, -jnp.inf, logits)
    v2 = jnp.max(rest, axis=0, keepdims=True)
    i2 = jnp.min(jnp.where(rest == v2, eidx, N_EXPERTS), axis=0, keepdims=True)
    sel2 = eidx == i2
    e2 = jnp.exp(v2 - v1)
    p1 = 1.0 / (1.0 + e2)
    p2 = e2 / (1.0 + e2)
    gate_t = jnp.where(sel1, p1, 0.0) + jnp.where(sel2, p2, 0.0)
    sel = jnp.where(sel1 | sel2, 1.0, 0.0)
    r = lax.broadcasted_iota(jnp.int32, (CUM_W, CUM_W), 0)
    c = lax.broadcasted_iota(jnp.int32, (CUM_W, CUM_W), 1)
    upper = jnp.where(r <= c, 1.0, 0.0).astype(BF16)
    carry = jnp.zeros((N_EXPERTS, 1), F32)
    pieces = []
    for b in range(tm // CUM_W):
        blk = sel[:, b * CUM_W:(b + 1) * CUM_W].astype(BF16)
        cs = jnp.dot(blk, upper, preferred_element_type=F32) + carry
        pieces.append(cs)
        carry = cs[:, CUM_W - 1:CUM_W]
    incl = jnp.concatenate(pieces, axis=1)
    pos_t = jnp.where(sel > 0.0, incl - 1.0, -1.0)
    post_ref[...] = pos_t.astype(jnp.int32)
    gatet_ref[...] = gate_t
    cnt_ref[...] = jnp.broadcast_to(carry.astype(jnp.int32)[None], cnt_ref.shape)
    padded = jnp.concatenate([pos_t, jnp.full((LANES - N_EXPERTS, tm), -1.0, F32)], axis=0)
    posc_ref[...] = padded.T.astype(jnp.int32)


def _router(x, g, router_t):
    T = x.shape[0]
    nt = T // MOE_TM
    return pl.pallas_call(
        _router_kernel,
        grid=(nt,),
        in_specs=[
            pl.BlockSpec((MOE_TM, D_MODEL), lambda i: (i, 0)),
            pl.BlockSpec((1, D_MODEL), lambda i: (0, 0)),
            pl.BlockSpec((N_EXPERTS, D_MODEL), lambda i: (0, 0)),
        ],
        out_specs=[
            pl.BlockSpec((N_EXPERTS, MOE_TM), lambda i: (0, i)),
            pl.BlockSpec((MOE_TM, LANES), lambda i: (i, 0)),
            pl.BlockSpec((N_EXPERTS, MOE_TM), lambda i: (0, i)),
            pl.BlockSpec((1, N_EXPERTS, LANES), lambda i: (i, 0, 0)),
        ],
        out_shape=[
            jax.ShapeDtypeStruct((N_EXPERTS, T), jnp.int32),
            jax.ShapeDtypeStruct((T, LANES), jnp.int32),
            jax.ShapeDtypeStruct((N_EXPERTS, T), F32),
            jax.ShapeDtypeStruct((nt, N_EXPERTS, LANES), jnp.int32),
        ],
        compiler_params=_cparams(("parallel",)),
        name="moe_router",
    )(x, g, router_t)


def _moe_kernel(cnt_ref, x_ref, g_ref, post_ref, posc_ref, gatet_ref, wg_ref, wu_ref, wd_ref,
                out_ref, h_scr, xs_scr, ys_scr, gc_scr):
    i = pl.program_id(0)
    e = pl.program_id(1)
    f = pl.program_id(2)
    tm = x_ref.shape[0]
    nblk = (cnt_ref[i * N_EXPERTS + e] + MOE_RB - 1) // MOE_RB

    @pl.when((e == 0) & (f == 0))
    def _():
        x = x_ref[...]
        h_scr[...] = _rms(x, g_ref[...]).astype(BF16)
        out_ref[...] = x

    @pl.when(f == 0)
    def _():
        pos_row = post_ref[pl.ds(e, 1), :]
        gate_row = gatet_ref[pl.ds(e, 1), :]
        slot = lax.broadcasted_iota(jnp.int32, (MOE_RB, tm), 0)

        def gather(r, _):
            hit = pos_row == slot + r * MOE_RB
            onehot = jnp.where(hit, 1.0, 0.0).astype(BF16)
            xs_scr[r] = jnp.dot(onehot, h_scr[...],
                                preferred_element_type=F32).astype(BF16)
            gsel = jnp.sum(jnp.where(hit, gate_row, 0.0), axis=-1, keepdims=True)
            gc_scr[r] = jnp.broadcast_to(gsel, gc_scr.shape[1:])
            return 0

        lax.fori_loop(0, nblk, gather, 0)

    def expert(r, _):
        xs = xs_scr[r]
        gate = jnp.dot(xs, wg_ref[...], preferred_element_type=F32)
        up = jnp.dot(xs, wu_ref[...], preferred_element_type=F32)
        hid = (gate * _sigmoid(gate) * up).astype(BF16)
        y = jnp.dot(hid, wd_ref[...], preferred_element_type=F32)

        @pl.when(f == 0)
        def _():
            ys_scr[r] = y

        @pl.when(f != 0)
        def _():
            ys_scr[r] = ys_scr[r] + y

        return 0

    lax.fori_loop(0, nblk, expert, 0)

    @pl.when(f == MOE_NF - 1)
    def _():
        lane = lax.broadcasted_iota(jnp.int32, posc_ref.shape, 1)
        pos_col = jnp.sum(jnp.where(lane == e, posc_ref[...], 0), axis=-1, keepdims=True)
        slot = lax.broadcasted_iota(jnp.int32, (tm, MOE_RB), 1)

        def scatter(r, _):
            onehot_t = jnp.where(pos_col == slot + r * MOE_RB, 1.0, 0.0).astype(BF16)
            ysg = (ys_scr[r] * gc_scr[r][:, 0:1]).astype(BF16)
            out_ref[...] += jnp.dot(onehot_t, ysg, preferred_element_type=F32)
            return 0

        lax.fori_loop(0, nblk, scatter, 0)


def _moe(x, g, pos_t, pos_c, gate_t, counts, wg, wu, wd):
    T = x.shape[0]
    nt = T // MOE_TM
    tf = F_PAD // MOE_NF
    nrb = MOE_TM // MOE_RB
    grid_spec = pltpu.PrefetchScalarGridSpec(
        num_scalar_prefetch=1,
        grid=(nt, N_EXPERTS, MOE_NF),
        in_specs=[
            pl.BlockSpec((MOE_TM, D_MODEL), lambda i, e, f, c: (i, 0)),
            pl.BlockSpec((1, D_MODEL), lambda i, e, f, c: (0, 0)),
            pl.BlockSpec((N_EXPERTS, MOE_TM), lambda i, e, f, c: (0, i)),
            pl.BlockSpec((MOE_TM, LANES), lambda i, e, f, c: (i, 0)),
            pl.BlockSpec((N_EXPERTS, MOE_TM), lambda i, e, f, c: (0, i)),
            pl.BlockSpec((None, D_MODEL, tf), lambda i, e, f, c: (e, 0, f)),
            pl.BlockSpec((None, D_MODEL, tf), lambda i, e, f, c: (e, 0, f)),
            pl.BlockSpec((None, tf, D_MODEL), lambda i, e, f, c: (e, f, 0)),
        ],
        out_specs=pl.BlockSpec((MOE_TM, D_MODEL), lambda i, e, f, c: (i, 0)),
        scratch_shapes=[
            pltpu.VMEM((MOE_TM, D_MODEL), BF16),
            pltpu.VMEM((nrb, MOE_RB, D_MODEL), BF16),
            pltpu.VMEM((nrb, MOE_RB, D_MODEL), F32),
            pltpu.VMEM((nrb, MOE_RB, LANES), F32),
        ],
    )
    return pl.pallas_call(
        _moe_kernel,
        grid_spec=grid_spec,
        out_shape=jax.ShapeDtypeStruct((T, D_MODEL), F32),
        compiler_params=_cparams(("parallel", "arbitrary", "arbitrary")),
        name="moe_experts",
    )(counts, x, g, pos_t, pos_c, gate_t, wg, wu, wd)


def _norm_kernel(x_ref, g_ref, out_ref):
    out_ref[...] = _rms(x_ref[...], g_ref[...])


def _final_norm(x, g):
    T = x.shape[0]
    tm = 1024
    return pl.pallas_call(
        _norm_kernel,
        grid=(T // tm,),
        in_specs=[pl.BlockSpec((tm, D_MODEL), lambda i: (i, 0)),
                  pl.BlockSpec((1, D_MODEL), lambda i: (0, 0))],
        out_specs=pl.BlockSpec((tm, D_MODEL), lambda i: (i, 0)),
        out_shape=jax.ShapeDtypeStruct((T, D_MODEL), F32),
        compiler_params=_cparams(("parallel",)),
        name="final_norm",
    )(x, g)


def _rope_tables(smax):
    d = A_HEAD_DIM
    inv = 1.0 / (ROPE_THETA ** (jnp.arange(0, d, 2, dtype=F32) / d))
    ang = jnp.arange(smax, dtype=F32)[:, None] * inv[None, :]
    cos, sin = jnp.cos(ang), jnp.sin(ang)
    reps = LANES // d
    cos_t = jnp.tile(jnp.concatenate([cos, cos], axis=-1), (1, reps))
    sin_t = jnp.tile(jnp.concatenate([-sin, sin], axis=-1), (1, reps))
    return cos_t, sin_t


def _split_gu(w_gu):
    pad = [(0, 0)] * (w_gu.ndim - 1) + [(0, F_PAD - D_FF)]
    wg = jnp.pad(w_gu[..., :D_FF], pad).astype(BF16)
    wu = jnp.pad(w_gu[..., D_FF:], pad).astype(BF16)
    return wg, wu


def _pad_down(w_down):
    pad = [(0, 0)] * (w_down.ndim - 2) + [(0, F_PAD - D_FF), (0, 0)]
    return jnp.pad(w_down, pad).astype(BF16)


def kernel(x_prompt, x_sample, norm_mix_g, w_in, conv_qk, b_igate, b_fgate, m_norm_g, lam_q1, lam_k1, lam_q2, lam_k2, a_norm_g, w_br_m, w_br_a, w_out, norm_ffn_g, dense_w_gu, dense_w_down, moe_router, moe_w_gu, moe_w_down, final_norm_g):
    bp, sp, _ = x_prompt.shape
    bs, ss, _ = x_sample.shape
    tp, ts = bp * sp, bs * ss
    kb = max(sp, ss)
    assert sp % ss == 0 and tp % kb == 0 and ts % kb == 0 and ss % M_CHUNK == 0
    assert tp % MOE_TM == 0 and ts % MOE_TM == 0
    geom = (tp, sp, ss)
    depth = w_in.shape[0]
    n_gate = 4 * M_HEADS
    g0 = 4 * D_MODEL

    x = jnp.concatenate([x_prompt.reshape(tp, D_MODEL), x_sample.reshape(ts, D_MODEL)], axis=0)
    cos_t, sin_t = _rope_tables(kb)
    row = lambda v: v.reshape(1, -1).astype(F32)

    for l in range(depth):
        lam_init = 0.8 - 0.6 * math.exp(-0.3 * l)
        w_main = jnp.concatenate([w_in[l][:, :g0], w_in[l][:, g0 + n_gate:]], axis=1).astype(BF16)
        w_gate = jnp.pad(w_in[l][:, g0:g0 + n_gate], ((0, 0), (0, LANES - n_gate))).astype(BF16)
        b_gate = jnp.pad(jnp.concatenate([b_igate[l].reshape(-1), b_fgate[l].reshape(-1)]),
                         (0, LANES - n_gate)).reshape(1, LANES).astype(F32)
        proj, gates = _inproj(x, row(norm_mix_g[l]), w_main, w_gate, b_gate, cos_t, sin_t, geom)

        qk = _conv_silu(proj, conv_qk[l].astype(F32), geom)
        h_fwd = _mlstm(qk, proj, gates, geom)
        hm = _mlstm(qk, proj, gates, geom, h_fwd=h_fwd, m_norm_g=row(m_norm_g[l]))

        lam_vecs = jnp.pad(jnp.stack([lam_q1[l], lam_k1[l], lam_q2[l], lam_k2[l]]).astype(F32),
                           ((0, 4), (0, LANES - A_HEAD_DIM)))
        ha = _attention(proj, lam_vecs, row(a_norm_g[l]), geom, lam_init)

        x = _merge(x, hm, ha, proj, w_br_m[l].astype(BF16), w_br_a[l].astype(BF16),
                   w_out[l].astype(BF16))

        if l % 2 == 0:
            wg, wu = _split_gu(dense_w_gu[l // 2])
            x = _ffn(x, row(norm_ffn_g[l]), wg, wu, _pad_down(dense_w_down[l // 2]))
        else:
            g = row(norm_ffn_g[l])
            pos_t, pos_c, gate_t, cnt = _router(x, g, moe_router[l // 2].T.astype(F32))
            wg, wu = _split_gu(moe_w_gu[l // 2])
            x = _moe(x, g, pos_t, pos_c, gate_t, cnt[:, :, 0].reshape(-1), wg, wu,
                     _pad_down(moe_w_down[l // 2]))

    y = _final_norm(x, row(final_norm_g))
    return (y[:tp].reshape(bp, sp, D_MODEL), y[tp:].reshape(bs, ss, D_MODEL))
```

```python
import functools
import math

import jax
import jax.numpy as jnp
from jax import lax
from jax.experimental import pallas as pl
from jax.experimental.pallas import tpu as pltpu

F32 = jnp.float32
BF16 = jnp.bfloat16

D_MODEL = 1024
M_HEADS = 4
M_HEAD_DIM = 256
M_CHUNK = 128
CONV_W = 5
A_HEADS = 8
A_HEAD_DIM = 64
A_VDIM = 2 * A_HEAD_DIM
D_FF = 2752
N_EXPERTS = 8
EPS = 1e-6
ROPE_THETA = 10000.0
LOG2_E = math.log2(math.e)

LANES = 128
F_PAD = 2816
N_MAIN_BLOCKS = 9
BLK_MQ, BLK_MK, BLK_MV, BLK_MO, BLK_AQ, BLK_AK, BLK_AV, BLK_GM, BLK_GA = range(9)
VMEM_LIMIT = 56 * 1024 * 1024


def _cparams(sem):
    return pltpu.CompilerParams(dimension_semantics=sem, vmem_limit_bytes=VMEM_LIMIT)


def _rms(x, g):
    return x * lax.rsqrt(jnp.mean(x * x, axis=-1, keepdims=True) + EPS) * g


def _sigmoid(x):
    return 0.5 * jnp.tanh(0.5 * x) + 0.5


def _seq_pos(row0, tp, sp, ss):
    in_prompt = row0 < tp
    pos = jnp.where(in_prompt, row0 % sp, (row0 - tp) % ss)
    slen = jnp.where(in_prompt, sp, ss)
    return pos, slen


def _inproj_kernel(x_ref, g_ref, w_ref, wgate_ref, bgate_ref, cos_ref, sin_ref,
                   out_ref, gates_ref, h_scr):
    j = pl.program_id(1)

    @pl.when(j == 0)
    def _():
        hb = _rms(x_ref[...], g_ref[...]).astype(BF16)
        h_scr[...] = hb
        gt = jnp.dot(hb, wgate_ref[...], preferred_element_type=F32) + bgate_ref[...]
        lane = lax.broadcasted_iota(jnp.int32, gt.shape, 1)
        log_sig = jnp.minimum(gt, 0.0) - jnp.log1p(jnp.exp(-jnp.abs(gt)))
        is_fgate = (lane >= 2 * M_HEADS) & (lane < 4 * M_HEADS)
        gates_ref[...] = jnp.where(is_fgate, log_sig, gt)

    acc = jnp.dot(h_scr[...], w_ref[...], preferred_element_type=F32)
    is_sig = (j == BLK_MO) | (j == BLK_GM) | (j == BLK_GA)
    is_rope = (j == BLK_AQ) | (j == BLK_AK)

    @pl.when(is_sig)
    def _():
        out_ref[...] = _sigmoid(acc).astype(out_ref.dtype)

    @pl.when(is_rope)
    def _():
        scale = jnp.where(j == BLK_AQ, A_HEAD_DIM ** -0.5 * LOG2_E, 1.0)
        cos = cos_ref[...] * scale
        sin = sin_ref[...] * scale
        for c in range(acc.shape[1] // LANES):
            cs = slice(c * LANES, (c + 1) * LANES)
            xc = acc[:, cs]
            roped = xc * cos + pltpu.roll(xc, LANES // 2, axis=1) * sin
            out_ref[:, cs] = roped.astype(out_ref.dtype)

    @pl.when(jnp.logical_not(is_sig | is_rope))
    def _():
        out_ref[...] = acc.astype(out_ref.dtype)


def _inproj(x, g, w_main, w_gate, b_gate, cos_t, sin_t, geom):
    T = x.shape[0]
    tp, sp, ss = geom
    tm = min(1024, ss)

    def pos_block(i, j):
        pos, _ = _seq_pos(i * tm, tp, sp, ss)
        return (pos // tm, 0)

    return pl.pallas_call(
        _inproj_kernel,
        grid=(T // tm, N_MAIN_BLOCKS),
        in_specs=[
            pl.BlockSpec((tm, D_MODEL), lambda i, j: (i, 0)),
            pl.BlockSpec((1, D_MODEL), lambda i, j: (0, 0)),
            pl.BlockSpec((D_MODEL, D_MODEL), lambda i, j: (0, j)),
            pl.BlockSpec((D_MODEL, LANES), lambda i, j: (0, 0)),
            pl.BlockSpec((1, LANES), lambda i, j: (0, 0)),
            pl.BlockSpec((tm, LANES), pos_block),
            pl.BlockSpec((tm, LANES), pos_block),
        ],
        out_specs=[
            pl.BlockSpec((tm, D_MODEL), lambda i, j: (i, j)),
            pl.BlockSpec((tm, LANES), lambda i, j: (i, 0)),
        ],
        out_shape=[
            jax.ShapeDtypeStruct((T, N_MAIN_BLOCKS * D_MODEL), BF16),
            jax.ShapeDtypeStruct((T, LANES), F32),
        ],
        scratch_shapes=[pltpu.VMEM((tm, D_MODEL), BF16)],
        compiler_params=_cparams(("parallel", "arbitrary")),
        name="inproj",
    )(x, g, w_main, w_gate, b_gate, cos_t, sin_t)


HALO = 16


def _conv_kernel(prev_ref, main_ref, next_ref, w_ref, out_ref, ext_scr, *, geom, rows):
    tp, sp, ss = geom
    i = pl.program_id(0)
    j = pl.program_id(1)
    pos, slen = _seq_pos(i * rows, tp, sp, ss)
    prev_ok = jnp.where(pos == 0, 0.0, 1.0)
    next_ok = jnp.where(pos + rows == slen, 0.0, 1.0)
    ext_scr[0:HALO, :] = prev_ref[...].astype(F32) * prev_ok
    ext_scr[HALO:HALO + rows, :] = main_ref[...].astype(F32)
    ext_scr[HALO + rows:, :] = next_ref[...].astype(F32) * next_ok
    pad = CONV_W // 2
    acc = None
    for t in range(CONV_W):
        term = ext_scr[HALO - pad + t:HALO - pad + t + rows, :] * w_ref[t:t + 1, :]
        acc = term if acc is None else acc + term
    y = acc * _sigmoid(acc)
    kscale = jnp.where(j == 1, M_HEAD_DIM ** -0.5, 1.0)
    out_ref[...] = (y * kscale).astype(out_ref.dtype)


def _conv_silu(proj, conv_w, geom):
    T = proj.shape[0]
    tp, sp, ss = geom
    rows = min(512, ss)
    hb = rows // HALO
    nh = T // HALO
    return pl.pallas_call(
        functools.partial(_conv_kernel, geom=geom, rows=rows),
        grid=(T // rows, 2),
        in_specs=[
            pl.BlockSpec((HALO, D_MODEL), lambda i, j: (jnp.maximum(i * hb - 1, 0), j)),
            pl.BlockSpec((rows, D_MODEL), lambda i, j: (i, j)),
            pl.BlockSpec((HALO, D_MODEL), lambda i, j: (jnp.minimum((i + 1) * hb, nh - 1), j)),
            pl.BlockSpec((CONV_W, D_MODEL), lambda i, j: (0, j)),
        ],
        out_specs=pl.BlockSpec((rows, D_MODEL), lambda i, j: (i, j)),
        out_shape=jax.ShapeDtypeStruct((T, 2 * D_MODEL), BF16),
        scratch_shapes=[pltpu.VMEM((rows + 2 * HALO, D_MODEL), F32)],
        compiler_params=_cparams(("parallel", "parallel")),
        name="conv_silu",
    )(proj, proj, proj, conv_w)


def _mlstm_kernel(*refs, reverse, geom):
    tp, sp, ss = geom
    if reverse:
        (q_ref, k_ref, v_ref, gates_ref, hf_ref, mo_ref, ng_ref,
         out_ref, c_scr, n_scr, m_scr) = refs
    else:
        q_ref, k_ref, v_ref, gates_ref, out_ref, c_scr, n_scr, m_scr = refs
    L = M_CHUNK
    dh = M_HEAD_DIM
    c = pl.program_id(0)
    chunk = (pl.num_programs(0) - 1 - c) if reverse else c
    pos, slen = _seq_pos(chunk * L, tp, sp, ss)
    first = (pos == slen - L) if reverse else (pos == 0)

    @pl.when(first)
    def _():
        c_scr[...] = jnp.zeros_like(c_scr)
        n_scr[...] = jnp.zeros_like(n_scr)
        m_scr[...] = jnp.zeros_like(m_scr)

    gates = gates_ref[...]
    row = lax.broadcasted_iota(jnp.int32, (L, L), 0)
    col = lax.broadcasted_iota(jnp.int32, (L, L), 1)
    valid = (col >= row) if reverse else (col <= row)
    tri = jnp.where(valid, 1.0, 0.0).astype(BF16)
    g1 = gates.astype(BF16)
    r1 = gates - g1.astype(F32)
    g2 = r1.astype(BF16)
    g3 = (r1 - g2.astype(F32)).astype(BF16)
    cum = (jnp.dot(tri, g1, preferred_element_type=F32)
           + jnp.dot(tri, g2, preferred_element_type=F32)
           + jnp.dot(tri, g3, preferred_element_type=F32))
    cum_t = cum.T
    gates_t = gates.T
    last = 0 if reverse else L - 1

    for h in range(M_HEADS):
        ci = (M_HEADS if reverse else 0) + h
        cf = 2 * M_HEADS + ci
        hs = slice(h * dh, (h + 1) * dh)
        b_col = cum[:, cf:cf + 1]
        b_row = cum_t[cf:cf + 1, :]
        i_col = gates[:, ci:ci + 1]
        i_row = gates_t[ci:ci + 1, :]
        m_prev = m_scr[h, 0:1, 0:1]
        n_prev = n_scr[h]
        c_prev = c_scr[h]
        q = q_ref[:, hs]
        k = k_ref[:, hs]
        v = v_ref[:, hs]

        logd = jnp.where(valid, b_col - b_row + i_row, -jnp.inf)
        m_inter = b_col + m_prev
        m_t = jnp.maximum(m_inter, jnp.max(logd, axis=-1, keepdims=True))
        w_inter = jnp.exp(m_inter - m_t)
        s = lax.dot_general(q, k, (((1,), (1,)), ((), ())),
                            preferred_element_type=F32) * jnp.exp(logd - m_t)
        num = (w_inter * jnp.dot(q, c_prev.astype(BF16), preferred_element_type=F32)
               + jnp.dot(s.astype(BF16), v, preferred_element_type=F32))
        den = (w_inter * jnp.sum(q.astype(F32) * n_prev, axis=-1, keepdims=True)
               + jnp.sum(s, axis=-1, keepdims=True))
        hout = num / jnp.maximum(jnp.abs(den), jnp.exp(-m_t))

        g_tot = cum[last:last + 1, cf:cf + 1]
        r_col = g_tot - b_col + i_col
        r_row = g_tot - b_row + i_row
        m_new = jnp.maximum(g_tot + m_prev, jnp.max(r_row, axis=-1, keepdims=True))
        decay = jnp.exp(g_tot + m_prev - m_new)
        kw = k.astype(F32) * jnp.exp(r_col - m_new)
        c_scr[h] = decay * c_prev + jnp.dot(kw.T.astype(BF16), v,
                                            preferred_element_type=F32)
        n_scr[h] = decay * n_prev + jnp.sum(kw, axis=0, keepdims=True)
        m_scr[h] = jnp.broadcast_to(m_new, m_scr.shape[1:])

        if reverse:
            hsum = hout + hf_ref[:, hs]
            normed = _rms(hsum, ng_ref[:, hs])
            out_ref[:, hs] = (normed * mo_ref[:, hs].astype(F32)).astype(out_ref.dtype)
        else:
            out_ref[:, hs] = hout


def _mlstm(qk, proj, gates, geom, h_fwd=None, m_norm_g=None):
    T = qk.shape[0]
    reverse = h_fwd is not None
    nc = T // M_CHUNK
    cm = (lambda c: nc - 1 - c) if reverse else (lambda c: c)
    in_specs = [
        pl.BlockSpec((M_CHUNK, D_MODEL), lambda c: (cm(c), 0)),
        pl.BlockSpec((M_CHUNK, D_MODEL), lambda c: (cm(c), 1)),
        pl.BlockSpec((M_CHUNK, D_MODEL), lambda c: (cm(c), BLK_MV)),
        pl.BlockSpec((M_CHUNK, LANES), lambda c: (cm(c), 0)),
    ]
    args = [qk, qk, proj, gates]
    if reverse:
        in_specs += [
            pl.BlockSpec((M_CHUNK, D_MODEL), lambda c: (cm(c), 0)),
            pl.BlockSpec((M_CHUNK, D_MODEL), lambda c: (cm(c), BLK_MO)),
            pl.BlockSpec((1, D_MODEL), lambda c: (0, 0)),
        ]
        args += [h_fwd, proj, m_norm_g]
    return pl.pallas_call(
        functools.partial(_mlstm_kernel, reverse=reverse, geom=geom),
        grid=(nc,),
        in_specs=in_specs,
        out_specs=pl.BlockSpec((M_CHUNK, D_MODEL), lambda c: (cm(c), 0)),
        out_shape=jax.ShapeDtypeStruct((T, D_MODEL), BF16 if reverse else F32),
        scratch_shapes=[
            pltpu.VMEM((M_HEADS, M_HEAD_DIM, M_HEAD_DIM), F32),
            pltpu.VMEM((M_HEADS, 1, M_HEAD_DIM), F32),
            pltpu.VMEM((M_HEADS, 8, LANES), F32),
        ],
        compiler_params=_cparams(("arbitrary",)),
        name="mlstm_bwd" if reverse else "mlstm_fwd",
    )(*args)


ATT_RS = 16
A_VEXT = A_VDIM + 16


def _attn_kernel(q_ref, k_ref, v_ref, lam_ref, ng_ref, out_ref, s_scr, p_scr, acc_scr,
                 *, geom, tq, tk, kb, lam_init):
    tp, sp, ss = geom
    r0 = pl.program_id(1) * tq
    pos, slen = _seq_pos(r0, tp, sp, ss)
    off = (r0 - pos) % kb
    nk = slen // tk
    d = A_HEAD_DIM
    q = q_ref[...]
    lane = lax.broadcasted_iota(jnp.int32, q.shape, 1)
    zero = jnp.zeros_like(q)
    first = (lane // (d // 2)) % 2 == 0
    qs = (jnp.where(first, q, zero), jnp.where(first, zero, q))
    acc_scr[...] = jnp.zeros_like(acc_scr)
    nt = (((1,), (1,)), ((), ()))

    def scores(t, slot):
        start = pl.multiple_of(off + t * tk, tk)
        kc = k_ref[pl.ds(start, tk), :]
        mx = []
        for st in range(2):
            s_t = lax.dot_general(kc, qs[st], nt, preferred_element_type=F32)
            s_scr[slot, st] = s_t
            mx.append(jnp.max(s_t, axis=0, keepdims=True))
        return tuple(mx)

    def accumulate(t, slot, mx, ms):
        vt = v_ref[off // tk + t]
        new_m = []
        for st in range(2):
            m_new = jnp.maximum(ms[st], mx[st])
            alpha = jnp.exp2(ms[st] - m_new)
            m_blk = jnp.broadcast_to(m_new, (ATT_RS, tq))
            for kb in range(tk // ATT_RS):
                rows = slice(kb * ATT_RS, (kb + 1) * ATT_RS)
                p = jnp.exp2(s_scr[slot, st, rows, :] - m_blk)
                p_scr[st, rows, :] = p.astype(BF16)
            new_m.append(m_new)
            acc_scr[st] = alpha * acc_scr[st] + jnp.dot(vt, p_scr[st],
                                                        preferred_element_type=F32)
        return tuple(new_m)

    def body(u, carry):
        mx, ms = carry
        t = 2 * u
        mx_b = scores(t + 1, 1)
        ms = accumulate(t, 0, mx, ms)
        mx_a = scores(t + 2, 0)
        ms = accumulate(t + 1, 1, mx_b, ms)
        return mx_a, ms

    neg = jnp.full((1, tq), -jnp.inf, F32)
    mx, ms = lax.fori_loop(0, nk // 2 - 1, body, (scores(0, 0), (neg, neg)))
    mx_b = scores(nk - 1, 1)
    ms = accumulate(nk - 2, 0, mx, ms)
    accumulate(nk - 1, 1, mx_b, ms)

    lv = lam_ref[...]
    lam = (jnp.exp(jnp.sum(lv[0:1, :] * lv[1:2, :], axis=-1, keepdims=True))
           - jnp.exp(jnp.sum(lv[2:3, :] * lv[3:4, :], axis=-1, keepdims=True)) + lam_init)
    l1 = acc_scr[0, A_VDIM:A_VDIM + 1, :]
    l2 = acc_scr[1, A_VDIM:A_VDIM + 1, :]
    o_t = acc_scr[0, :A_VDIM, :] / l1 - lam * (acc_scr[1, :A_VDIM, :] / l2)
    inv = lax.rsqrt(jnp.mean(o_t * o_t, axis=0, keepdims=True) + EPS)
    y_t = o_t * inv * ng_ref[...] * (1.0 - lam_init)
    out_ref[...] = y_t.T.astype(out_ref.dtype)


def _attention(proj, lam_vecs, a_norm_g, geom, lam_init):
    T = proj.shape[0]
    tp, sp, ss = geom
    kb = max(sp, ss)
    tq = min(512, ss)
    tk = min(512, ss // 2)
    assert ss % (2 * tk) == 0 and sp % (2 * tk) == 0
    hpb = D_MODEL // A_VDIM
    v_t = proj[:, BLK_AV * D_MODEL:(BLK_AV + 1) * D_MODEL].reshape(T // tk, tk, A_HEADS, A_VDIM)
    v_t = v_t.transpose(0, 2, 3, 1)
    extra = jnp.zeros((T // tk, A_HEADS, A_VEXT - A_VDIM, tk), BF16).at[:, :, 0, :].set(1.0)
    v_t = jnp.concatenate([v_t, extra], axis=2)
    return pl.pallas_call(
        functools.partial(_attn_kernel, geom=geom, tq=tq, tk=tk, kb=kb, lam_init=lam_init),
        grid=(A_HEADS, T // tq),
        in_specs=[
            pl.BlockSpec((tq, A_VDIM), lambda h, r: (r, BLK_AQ * hpb + h)),
            pl.BlockSpec((kb, A_VDIM), lambda h, r: ((r * tq) // kb, BLK_AK * hpb + h)),
            pl.BlockSpec((kb // tk, None, A_VEXT, tk), lambda h, r: ((r * tq) // kb, h, 0, 0)),
            pl.BlockSpec((8, LANES), lambda h, r: (0, 0)),
            pl.BlockSpec((A_VDIM, 1), lambda h, r: (0, 0)),
        ],
        out_specs=pl.BlockSpec((tq, A_VDIM), lambda h, r: (r, h)),
        out_shape=jax.ShapeDtypeStruct((T, D_MODEL), BF16),
        scratch_shapes=[
            pltpu.VMEM((2, 2, tk, tq), F32),
            pltpu.VMEM((2, tk, tq), BF16),
            pltpu.VMEM((2, A_VEXT, tq), F32),
        ],
        compiler_params=_cparams(("parallel", "parallel")),
        name="diff_attn",
    )(proj, proj, v_t, lam_vecs, a_norm_g)


def _merge_kernel(x_ref, hm_ref, ha_ref, gm_ref, ga_ref, wm_ref, wa_ref, wo_ref, out_ref):
    bm = jnp.dot(hm_ref[...], wm_ref[...], preferred_element_type=F32)
    ba = jnp.dot(ha_ref[...], wa_ref[...], preferred_element_type=F32)
    merged = gm_ref[...].astype(F32) * bm + ga_ref[...].astype(F32) * ba
    out_ref[...] = x_ref[...] + jnp.dot(merged.astype(BF16), wo_ref[...],
                                        preferred_element_type=F32)


def _merge(x, hm, ha, proj, w_m, w_a, w_o):
    T = x.shape[0]
    tm = 512
    row = lambda i: (i, 0)
    const = lambda i: (0, 0)
    wspec = pl.BlockSpec((D_MODEL, D_MODEL), const)
    return pl.pallas_call(
        _merge_kernel,
        grid=(T // tm,),
        in_specs=[
            pl.BlockSpec((tm, D_MODEL), row),
            pl.BlockSpec((tm, D_MODEL), row),
            pl.BlockSpec((tm, D_MODEL), row),
            pl.BlockSpec((tm, D_MODEL), lambda i: (i, BLK_GM)),
            pl.BlockSpec((tm, D_MODEL), lambda i: (i, BLK_GA)),
            wspec, wspec, wspec,
        ],
        out_specs=pl.BlockSpec((tm, D_MODEL), row),
        out_shape=jax.ShapeDtypeStruct((T, D_MODEL), F32),
        compiler_params=_cparams(("parallel",)),
        name="merge_out",
    )(x, hm, ha, proj, proj, w_m, w_a, w_o)


FFN_SUB = 256


def _ffn_kernel(x_ref, g_ref, wg_ref, wu_ref, wd_ref, out_ref):
    x = x_ref[...]
    hb = _rms(x, g_ref[...]).astype(BF16)
    acc = x
    for s in range(F_PAD // FFN_SUB):
        cs = slice(s * FFN_SUB, (s + 1) * FFN_SUB)
        gate = jnp.dot(hb, wg_ref[:, cs], preferred_element_type=F32)
        up = jnp.dot(hb, wu_ref[:, cs], preferred_element_type=F32)
        hid = (gate * _sigmoid(gate) * up).astype(BF16)
        acc = acc + jnp.dot(hid, wd_ref[cs, :], preferred_element_type=F32)
    out_ref[...] = acc


def _ffn(x, g, wg, wu, wd):
    T = x.shape[0]
    tm = 512
    row = lambda i: (i, 0)
    const = lambda i: (0, 0)
    once = pl.Buffered(1)
    return pl.pallas_call(
        _ffn_kernel,
        grid=(T // tm,),
        in_specs=[
            pl.BlockSpec((tm, D_MODEL), row),
            pl.BlockSpec((1, D_MODEL), const),
            pl.BlockSpec((D_MODEL, F_PAD), const, pipeline_mode=once),
            pl.BlockSpec((D_MODEL, F_PAD), const, pipeline_mode=once),
            pl.BlockSpec((F_PAD, D_MODEL), const, pipeline_mode=once),
        ],
        out_specs=pl.BlockSpec((tm, D_MODEL), row),
        out_shape=jax.ShapeDtypeStruct((T, D_MODEL), F32),
        compiler_params=_cparams(("parallel",)),
        name="ffn_dense",
    )(x, g, wg, wu, wd)


MOE_TM = 1024
MOE_RB = 256
MOE_NF = 2
CUM_W = 256


def _router_kernel(x_ref, g_ref, rt_ref, post_ref, posc_ref, gatet_ref, cnt_ref):
    tm = x_ref.shape[0]
    h = _rms(x_ref[...], g_ref[...])
    rt = rt_ref[...]
    h_hi = h.astype(BF16)
    h_lo = (h - h_hi.astype(F32)).astype(BF16)
    r_hi = rt.astype(BF16)
    r_lo = (rt - r_hi.astype(F32)).astype(BF16)
    nt = (((1,), (1,)), ((), ()))
    logits = (lax.dot_general(r_hi, h_hi, nt, preferred_element_type=F32)
              + lax.dot_general(r_hi, h_lo, nt, preferred_element_type=F32)
              + lax.dot_general(r_lo, h_hi, nt, preferred_element_type=F32))
    eidx = lax.broadcasted_iota(jnp.int32, logits.shape, 0)
    v1 = jnp.max(logits, axis=0, keepdims=True)
    i1 = jnp.min(jnp.where(logits == v1, eidx, N_EXPERTS), axis=0, keepdims=True)
    sel1 = eidx == i1
    rest = jnp.where(sel1, -jnp.inf, logits)
    v2 = jnp.max(rest, axis=0, keepdims=True)
    i2 = jnp.min(jnp.where(rest == v2, eidx, N_EXPERTS), axis=0, keepdims=True)
    sel2 = eidx == i2
    e2 = jnp.exp(v2 - v1)
    p1 = 1.0 / (1.0 + e2)
    p2 = e2 / (1.0 + e2)
    gate_t = jnp.where(sel1, p1, 0.0) + jnp.where(sel2, p2, 0.0)
    sel = jnp.where(sel1 | sel2, 1.0, 0.0)
    r = lax.broadcasted_iota(jnp.int32, (CUM_W, CUM_W), 0)
    c = lax.broadcasted_iota(jnp.int32, (CUM_W, CUM_W), 1)
    upper = jnp.where(r <= c, 1.0, 0.0).astype(BF16)
    carry = jnp.zeros((N_EXPERTS, 1), F32)
    pieces = []
    for b in range(tm // CUM_W):
        blk = sel[:, b * CUM_W:(b + 1) * CUM_W].astype(BF16)
        cs = jnp.dot(blk, upper, preferred_element_type=F32) + carry
        pieces.append(cs)
        carry = cs[:, CUM_W - 1:CUM_W]
    incl = jnp.concatenate(pieces, axis=1)
    pos_t = jnp.where(sel > 0.0, incl - 1.0, -1.0)
    post_ref[...] = pos_t.astype(jnp.int32)
    gatet_ref[...] = gate_t
    cnt_ref[...] = jnp.broadcast_to(carry.astype(jnp.int32)[None], cnt_ref.shape)
    padded = jnp.concatenate([pos_t, jnp.full((LANES - N_EXPERTS, tm), -1.0, F32)], axis=0)
    posc_ref[...] = padded.T.astype(jnp.int32)


def _router(x, g, router_t):
    T = x.shape[0]
    nt = T // MOE_TM
    return pl.pallas_call(
        _router_kernel,
        grid=(nt,),
        in_specs=[
            pl.BlockSpec((MOE_TM, D_MODEL), lambda i: (i, 0)),
            pl.BlockSpec((1, D_MODEL), lambda i: (0, 0)),
            pl.BlockSpec((N_EXPERTS, D_MODEL), lambda i: (0, 0)),
        ],
        out_specs=[
            pl.BlockSpec((N_EXPERTS, MOE_TM), lambda i: (0, i)),
            pl.BlockSpec((MOE_TM, LANES), lambda i: (i, 0)),
            pl.BlockSpec((N_EXPERTS, MOE_TM), lambda i: (0, i)),
            pl.BlockSpec((1, N_EXPERTS, LANES), lambda i: (i, 0, 0)),
        ],
        out_shape=[
            jax.ShapeDtypeStruct((N_EXPERTS, T), jnp.int32),
            jax.ShapeDtypeStruct((T, LANES), jnp.int32),
            jax.ShapeDtypeStruct((N_EXPERTS, T), F32),
            jax.ShapeDtypeStruct((nt, N_EXPERTS, LANES), jnp.int32),
        ],
        compiler_params=_cparams(("parallel",)),
        name="moe_router",
    )(x, g, router_t)


def _moe_kernel(cnt_ref, x_ref, g_ref, post_ref, posc_ref, gatet_ref, wg_ref, wu_ref, wd_ref,
                out_ref, h_scr, xs_scr, ys_scr, gc_scr):
    i = pl.program_id(0)
    e = pl.program_id(1)
    f = pl.program_id(2)
    tm = x_ref.shape[0]
    nblk = (cnt_ref[i * N_EXPERTS + e] + MOE_RB - 1) // MOE_RB

    @pl.when((e == 0) & (f == 0))
    def _():
        x = x_ref[...]
        h_scr[...] = _rms(x, g_ref[...]).astype(BF16)
        out_ref[...] = x

    @pl.when(f == 0)
    def _():
        pos_row = post_ref[pl.ds(e, 1), :]
        gate_row = gatet_ref[pl.ds(e, 1), :]
        slot = lax.broadcasted_iota(jnp.int32, (MOE_RB, tm), 0)

        def gather(r, _):
            hit = pos_row == slot + r * MOE_RB
            onehot = jnp.where(hit, 1.0, 0.0).astype(BF16)
            xs_scr[r] = jnp.dot(onehot, h_scr[...],
                                preferred_element_type=F32).astype(BF16)
            gsel = jnp.sum(jnp.where(hit, gate_row, 0.0), axis=-1, keepdims=True)
            gc_scr[r] = jnp.broadcast_to(gsel, gc_scr.shape[1:])
            return 0

        lax.fori_loop(0, nblk, gather, 0)

    def expert(r, _):
        xs = xs_scr[r]
        gate = jnp.dot(xs, wg_ref[...], preferred_element_type=F32)
        up = jnp.dot(xs, wu_ref[...], preferred_element_type=F32)
        hid = (gate * _sigmoid(gate) * up).astype(BF16)
        y = jnp.dot(hid, wd_ref[...], preferred_element_type=F32)

        @pl.when(f == 0)
        def _():
            ys_scr[r] = y

        @pl.when(f != 0)
        def _():
            ys_scr[r] = ys_scr[r] + y

        return 0

    lax.fori_loop(0, nblk, expert, 0)

    @pl.when(f == MOE_NF - 1)
    def _():
        lane = lax.broadcasted_iota(jnp.int32, posc_ref.shape, 1)
        pos_col = jnp.sum(jnp.where(lane == e, posc_ref[...], 0), axis=-1, keepdims=True)
        slot = lax.broadcasted_iota(jnp.int32, (tm, MOE_RB), 1)

        def scatter(r, _):
            onehot_t = jnp.where(pos_col == slot + r * MOE_RB, 1.0, 0.0).astype(BF16)
            ysg = (ys_scr[r] * gc_scr[r][:, 0:1]).astype(BF16)
            out_ref[...] += jnp.dot(onehot_t, ysg, preferred_element_type=F32)
            return 0

        lax.fori_loop(0, nblk, scatter, 0)


def _moe(x, g, pos_t, pos_c, gate_t, counts, wg, wu, wd):
    T = x.shape[0]
    nt = T // MOE_TM
    tf = F_PAD // MOE_NF
    nrb = MOE_TM // MOE_RB
    grid_spec = pltpu.PrefetchScalarGridSpec(
        num_scalar_prefetch=1,
        grid=(nt, N_EXPERTS, MOE_NF),
        in_specs=[
            pl.BlockSpec((MOE_TM, D_MODEL), lambda i, e, f, c: (i, 0)),
            pl.BlockSpec((1, D_MODEL), lambda i, e, f, c: (0, 0)),
            pl.BlockSpec((N_EXPERTS, MOE_TM), lambda i, e, f, c: (0, i)),
            pl.BlockSpec((MOE_TM, LANES), lambda i, e, f, c: (i, 0)),
            pl.BlockSpec((N_EXPERTS, MOE_TM), lambda i, e, f, c: (0, i)),
            pl.BlockSpec((None, D_MODEL, tf), lambda i, e, f, c: (e, 0, f)),
            pl.BlockSpec((None, D_MODEL, tf), lambda i, e, f, c: (e, 0, f)),
            pl.BlockSpec((None, tf, D_MODEL), lambda i, e, f, c: (e, f, 0)),
        ],
        out_specs=pl.BlockSpec((MOE_TM, D_MODEL), lambda i, e, f, c: (i, 0)),
        scratch_shapes=[
            pltpu.VMEM((MOE_TM, D_MODEL), BF16),
            pltpu.VMEM((nrb, MOE_RB, D_MODEL), BF16),
            pltpu.VMEM((nrb, MOE_RB, D_MODEL), F32),
            pltpu.VMEM((nrb, MOE_RB, LANES), F32),
        ],
    )
    return pl.pallas_call(
        _moe_kernel,
        grid_spec=grid_spec,
        out_shape=jax.ShapeDtypeStruct((T, D_MODEL), F32),
        compiler_params=_cparams(("parallel", "arbitrary", "arbitrary")),
        name="moe_experts",
    )(counts, x, g, pos_t, pos_c, gate_t, wg, wu, wd)


def _norm_kernel(x_ref, g_ref, out_ref):
    out_ref[...] = _rms(x_ref[...], g_ref[...])


def _final_norm(x, g):
    T = x.shape[0]
    tm = 1024
    return pl.pallas_call(
        _norm_kernel,
        grid=(T // tm,),
        in_specs=[pl.BlockSpec((tm, D_MODEL), lambda i: (i, 0)),
                  pl.BlockSpec((1, D_MODEL), lambda i: (0, 0))],
        out_specs=pl.BlockSpec((tm, D_MODEL), lambda i: (i, 0)),
        out_shape=jax.ShapeDtypeStruct((T, D_MODEL), F32),
        compiler_params=_cparams(("parallel",)),
        name="final_norm",
    )(x, g)


def _rope_tables(smax):
    d = A_HEAD_DIM
    inv = 1.0 / (ROPE_THETA ** (jnp.arange(0, d, 2, dtype=F32) / d))
    ang = jnp.arange(smax, dtype=F32)[:, None] * inv[None, :]
    cos, sin = jnp.cos(ang), jnp.sin(ang)
    cos_t = jnp.concatenate([cos, cos, cos, cos], axis=-1)
    sin_t = jnp.concatenate([-sin, -sin, sin, sin], axis=-1)
    return cos_t, sin_t


def _rope_perm():
    half = A_HEAD_DIM // 2
    one = [0 * half, 2 * half, 1 * half, 3 * half]
    perm = [s + i for s in one for i in range(half)]
    return jnp.asarray([g * LANES + p for g in range(D_MODEL // LANES) for p in perm],
                       dtype=jnp.int32)


def _split_gu(w_gu):
    pad = [(0, 0)] * (w_gu.ndim - 1) + [(0, F_PAD - D_FF)]
    wg = jnp.pad(w_gu[..., :D_FF], pad).astype(BF16)
    wu = jnp.pad(w_gu[..., D_FF:], pad).astype(BF16)
    return wg, wu


def _pad_down(w_down):
    pad = [(0, 0)] * (w_down.ndim - 2) + [(0, F_PAD - D_FF), (0, 0)]
    return jnp.pad(w_down, pad).astype(BF16)


def kernel(x_prompt, x_sample, norm_mix_g, w_in, conv_qk, b_igate, b_fgate, m_norm_g, lam_q1, lam_k1, lam_q2, lam_k2, a_norm_g, w_br_m, w_br_a, w_out, norm_ffn_g, dense_w_gu, dense_w_down, moe_router, moe_w_gu, moe_w_down, final_norm_g):
    bp, sp, _ = x_prompt.shape
    bs, ss, _ = x_sample.shape
    tp, ts = bp * sp, bs * ss
    kb = max(sp, ss)
    assert sp % ss == 0 and tp % kb == 0 and ts % kb == 0 and ss % M_CHUNK == 0
    assert tp % MOE_TM == 0 and ts % MOE_TM == 0
    geom = (tp, sp, ss)
    depth = w_in.shape[0]
    n_gate = 4 * M_HEADS
    g0 = 4 * D_MODEL

    x = jnp.concatenate([x_prompt.reshape(tp, D_MODEL), x_sample.reshape(ts, D_MODEL)], axis=0)
    cos_t, sin_t = _rope_tables(kb)
    perm = _rope_perm()
    row = lambda v: v.reshape(1, -1).astype(F32)

    for l in range(depth):
        lam_init = 0.8 - 0.6 * math.exp(-0.3 * l)
        w_rest = w_in[l][:, g0 + n_gate:]
        w_main = jnp.concatenate(
            [w_in[l][:, :g0], w_rest[:, :D_MODEL][:, perm], w_rest[:, D_MODEL:2 * D_MODEL][:, perm],
             w_rest[:, 2 * D_MODEL:]], axis=1).astype(BF16)
        w_gate = jnp.pad(w_in[l][:, g0:g0 + n_gate], ((0, 0), (0, LANES - n_gate))).astype(BF16)
        b_gate = jnp.pad(jnp.concatenate([b_igate[l].reshape(-1), b_fgate[l].reshape(-1)]),
                         (0, LANES - n_gate)).reshape(1, LANES).astype(F32)
        proj, gates = _inproj(x, row(norm_mix_g[l]), w_main, w_gate, b_gate, cos_t, sin_t, geom)

        qk = _conv_silu(proj, conv_qk[l].astype(F32), geom)
        h_fwd = _mlstm(qk, proj, gates, geom)
        hm = _mlstm(qk, proj, gates, geom, h_fwd=h_fwd, m_norm_g=row(m_norm_g[l]))

        lam_vecs = jnp.pad(jnp.stack([lam_q1[l], lam_k1[l], lam_q2[l], lam_k2[l]]).astype(F32),
                           ((0, 4), (0, LANES - A_HEAD_DIM)))
        ha = _attention(proj, lam_vecs, a_norm_g[l].reshape(A_VDIM, 1).astype(F32), geom,
                        lam_init)

        x = _merge(x, hm, ha, proj, w_br_m[l].astype(BF16), w_br_a[l].astype(BF16),
                   w_out[l].astype(BF16))

        if l % 2 == 0:
            wg, wu = _split_gu(dense_w_gu[l // 2])
            x = _ffn(x, row(norm_ffn_g[l]), wg, wu, _pad_down(dense_w_down[l // 2]))
        else:
            g = row(norm_ffn_g[l])
            pos_t, pos_c, gate_t, cnt = _router(x, g, moe_router[l // 2].T.astype(F32))
            wg, wu = _split_gu(moe_w_gu[l // 2])
            x = _moe(x, g, pos_t, pos_c, gate_t, cnt[:, :, 0].reshape(-1), wg, wu,
                     _pad_down(moe_w_down[l // 2]))

    y = _final_norm(x, row(final_norm_g))
    return (y[:tp].reshape(bp, sp, D_MODEL), y[tp:].reshape(bs, ss, D_MODEL))
```

```python
import functools
import math

import jax
import jax.numpy as jnp
from jax import lax
from jax.experimental import pallas as pl
from jax.experimental.pallas import tpu as pltpu

F32 = jnp.float32
BF16 = jnp.bfloat16

D_MODEL = 1024
M_HEADS = 4
M_HEAD_DIM = 256
M_CHUNK = 128
CONV_W = 5
A_HEADS = 8
A_HEAD_DIM = 64
A_VDIM = 2 * A_HEAD_DIM
D_FF = 2752
N_EXPERTS = 8
EPS = 1e-6
ROPE_THETA = 10000.0
LOG2_E = math.log2(math.e)

LANES = 128
F_PAD = 2816
N_MAIN_BLOCKS = 9
BLK_MQ, BLK_MK, BLK_MV, BLK_MO, BLK_AQ, BLK_AK, BLK_AV, BLK_GM, BLK_GA = range(9)
VMEM_LIMIT = 56 * 1024 * 1024


def _cparams(sem):
    return pltpu.CompilerParams(dimension_semantics=sem, vmem_limit_bytes=VMEM_LIMIT)


def _rms(x, g):
    return x * lax.rsqrt(jnp.mean(x * x, axis=-1, keepdims=True) + EPS) * g


def _sigmoid(x):
    return 0.5 * jnp.tanh(0.5 * x) + 0.5


def _seq_pos(row0, tp, sp, ss):
    in_prompt = row0 < tp
    pos = jnp.where(in_prompt, row0 % sp, (row0 - tp) % ss)
    slen = jnp.where(in_prompt, sp, ss)
    return pos, slen


def _inproj_kernel(x_ref, g_ref, w_ref, wgate_ref, bgate_ref, cos_ref, sin_ref,
                   out_ref, gates_ref, h_scr):
    j = pl.program_id(1)

    @pl.when(j == 0)
    def _():
        hb = _rms(x_ref[...], g_ref[...]).astype(BF16)
        h_scr[...] = hb
        gt = jnp.dot(hb, wgate_ref[...], preferred_element_type=F32) + bgate_ref[...]
        lane = lax.broadcasted_iota(jnp.int32, gt.shape, 1)
        log_sig = jnp.minimum(gt, 0.0) - jnp.log1p(jnp.exp(-jnp.abs(gt)))
        is_fgate = (lane >= 2 * M_HEADS) & (lane < 4 * M_HEADS)
        gates_ref[...] = jnp.where(is_fgate, log_sig, gt)

    acc = jnp.dot(h_scr[...], w_ref[...], preferred_element_type=F32)
    is_sig = (j == BLK_MO) | (j == BLK_GM) | (j == BLK_GA)
    is_rope = (j == BLK_AQ) | (j == BLK_AK)

    @pl.when(is_sig)
    def _():
        out_ref[...] = _sigmoid(acc).astype(out_ref.dtype)

    @pl.when(is_rope)
    def _():
        scale = jnp.where(j == BLK_AQ, A_HEAD_DIM ** -0.5 * LOG2_E, 1.0)
        cos = cos_ref[...] * scale
        sin = sin_ref[...] * scale
        for c in range(acc.shape[1] // LANES):
            cs = slice(c * LANES, (c + 1) * LANES)
            xc = acc[:, cs]
            roped = xc * cos + pltpu.roll(xc, LANES // 2, axis=1) * sin
            out_ref[:, cs] = roped.astype(out_ref.dtype)

    @pl.when(jnp.logical_not(is_sig | is_rope))
    def _():
        out_ref[...] = acc.astype(out_ref.dtype)


def _inproj(x, g, w_main, w_gate, b_gate, cos_t, sin_t, geom):
    T = x.shape[0]
    tp, sp, ss = geom
    tm = min(1024, ss)

    def pos_block(i, j):
        pos, _ = _seq_pos(i * tm, tp, sp, ss)
        return (pos // tm, 0)

    return pl.pallas_call(
        _inproj_kernel,
        grid=(T // tm, N_MAIN_BLOCKS),
        in_specs=[
            pl.BlockSpec((tm, D_MODEL), lambda i, j: (i, 0)),
            pl.BlockSpec((1, D_MODEL), lambda i, j: (0, 0)),
            pl.BlockSpec((D_MODEL, D_MODEL), lambda i, j: (0, j)),
            pl.BlockSpec((D_MODEL, LANES), lambda i, j: (0, 0)),
            pl.BlockSpec((1, LANES), lambda i, j: (0, 0)),
            pl.BlockSpec((tm, LANES), pos_block),
            pl.BlockSpec((tm, LANES), pos_block),
        ],
        out_specs=[
            pl.BlockSpec((tm, D_MODEL), lambda i, j: (i, j)),
            pl.BlockSpec((tm, LANES), lambda i, j: (i, 0)),
        ],
        out_shape=[
            jax.ShapeDtypeStruct((T, N_MAIN_BLOCKS * D_MODEL), BF16),
            jax.ShapeDtypeStruct((T, LANES), F32),
        ],
        scratch_shapes=[pltpu.VMEM((tm, D_MODEL), BF16)],
        compiler_params=_cparams(("parallel", "arbitrary")),
        name="inproj",
    )(x, g, w_main, w_gate, b_gate, cos_t, sin_t)


HALO = 16


def _conv_kernel(prev_ref, main_ref, next_ref, w_ref, out_ref, ext_scr, *, geom, rows):
    tp, sp, ss = geom
    i = pl.program_id(0)
    j = pl.program_id(1)
    pos, slen = _seq_pos(i * rows, tp, sp, ss)
    prev_ok = jnp.where(pos == 0, 0.0, 1.0)
    next_ok = jnp.where(pos + rows == slen, 0.0, 1.0)
    ext_scr[0:HALO, :] = prev_ref[...].astype(F32) * prev_ok
    ext_scr[HALO:HALO + rows, :] = main_ref[...].astype(F32)
    ext_scr[HALO + rows:, :] = next_ref[...].astype(F32) * next_ok
    pad = CONV_W // 2
    acc = None
    for t in range(CONV_W):
        term = ext_scr[HALO - pad + t:HALO - pad + t + rows, :] * w_ref[t:t + 1, :]
        acc = term if acc is None else acc + term
    y = acc * _sigmoid(acc)
    kscale = jnp.where(j == 1, M_HEAD_DIM ** -0.5, 1.0)
    out_ref[...] = (y * kscale).astype(out_ref.dtype)


def _conv_silu(proj, conv_w, geom):
    T = proj.shape[0]
    tp, sp, ss = geom
    rows = min(512, ss)
    hb = rows // HALO
    nh = T // HALO
    return pl.pallas_call(
        functools.partial(_conv_kernel, geom=geom, rows=rows),
        grid=(T // rows, 2),
        in_specs=[
            pl.BlockSpec((HALO, D_MODEL), lambda i, j: (jnp.maximum(i * hb - 1, 0), j)),
            pl.BlockSpec((rows, D_MODEL), lambda i, j: (i, j)),
            pl.BlockSpec((HALO, D_MODEL), lambda i, j: (jnp.minimum((i + 1) * hb, nh - 1), j)),
            pl.BlockSpec((CONV_W, D_MODEL), lambda i, j: (0, j)),
        ],
        out_specs=pl.BlockSpec((rows, D_MODEL), lambda i, j: (i, j)),
        out_shape=jax.ShapeDtypeStruct((T, 2 * D_MODEL), BF16),
        scratch_shapes=[pltpu.VMEM((rows + 2 * HALO, D_MODEL), F32)],
        compiler_params=_cparams(("parallel", "parallel")),
        name="conv_silu",
    )(proj, proj, proj, conv_w)


def _mlstm_kernel(*refs, reverse, geom):
    tp, sp, ss = geom
    if reverse:
        (q_ref, k_ref, v_ref, gates_ref, hf_ref, mo_ref, ng_ref,
         out_ref, c_scr, n_scr, m_scr) = refs
    else:
        q_ref, k_ref, v_ref, gates_ref, out_ref, c_scr, n_scr, m_scr = refs
    L = M_CHUNK
    dh = M_HEAD_DIM
    c = pl.program_id(0)
    chunk = (pl.num_programs(0) - 1 - c) if reverse else c
    pos, slen = _seq_pos(chunk * L, tp, sp, ss)
    first = (pos == slen - L) if reverse else (pos == 0)

    @pl.when(first)
    def _():
        c_scr[...] = jnp.zeros_like(c_scr)
        n_scr[...] = jnp.zeros_like(n_scr)
        m_scr[...] = jnp.zeros_like(m_scr)

    gates = gates_ref[...]
    row = lax.broadcasted_iota(jnp.int32, (L, L), 0)
    col = lax.broadcasted_iota(jnp.int32, (L, L), 1)
    valid = (col >= row) if reverse else (col <= row)
    tri = jnp.where(valid, 1.0, 0.0).astype(BF16)
    g1 = gates.astype(BF16)
    r1 = gates - g1.astype(F32)
    g2 = r1.astype(BF16)
    g3 = (r1 - g2.astype(F32)).astype(BF16)
    cum = (jnp.dot(tri, g1, preferred_element_type=F32)
           + jnp.dot(tri, g2, preferred_element_type=F32)
           + jnp.dot(tri, g3, preferred_element_type=F32))
    cum_t = cum.T
    gates_t = gates.T
    last = 0 if reverse else L - 1

    for h in range(M_HEADS):
        ci = (M_HEADS if reverse else 0) + h
        cf = 2 * M_HEADS + ci
        hs = slice(h * dh, (h + 1) * dh)
        b_col = cum[:, cf:cf + 1]
        b_row = cum_t[cf:cf + 1, :]
        i_col = gates[:, ci:ci + 1]
        i_row = gates_t[ci:ci + 1, :]
        m_prev = m_scr[h, 0:1, 0:1]
        n_prev = n_scr[h]
        c_prev = c_scr[h]
        q = q_ref[:, hs]
        k = k_ref[:, hs]
        v = v_ref[:, hs]

        logd = jnp.where(valid, b_col - b_row + i_row, -jnp.inf)
        m_inter = b_col + m_prev
        m_t = jnp.maximum(m_inter, jnp.max(logd, axis=-1, keepdims=True))
        w_inter = jnp.exp(m_inter - m_t)
        s = lax.dot_general(q, k, (((1,), (1,)), ((), ())),
                            preferred_element_type=F32) * jnp.exp(logd - m_t)
        num = (w_inter * jnp.dot(q, c_prev.astype(BF16), preferred_element_type=F32)
               + jnp.dot(s.astype(BF16), v, preferred_element_type=F32))
        den = (w_inter * jnp.sum(q.astype(F32) * n_prev, axis=-1, keepdims=True)
               + jnp.sum(s, axis=-1, keepdims=True))
        hout = num / jnp.maximum(jnp.abs(den), jnp.exp(-m_t))

        g_tot = cum[last:last + 1, cf:cf + 1]
        r_col = g_tot - b_col + i_col
        r_row = g_tot - b_row + i_row
        m_new = jnp.maximum(g_tot + m_prev, jnp.max(r_row, axis=-1, keepdims=True))
        decay = jnp.exp(g_tot + m_prev - m_new)
        kw = k.astype(F32) * jnp.exp(r_col - m_new)
        c_scr[h] = decay * c_prev + jnp.dot(kw.T.astype(BF16), v,
                                            preferred_element_type=F32)
        n_scr[h] = decay * n_prev + jnp.sum(kw, axis=0, keepdims=True)
        m_scr[h] = jnp.broadcast_to(m_new, m_scr.shape[1:])

        if reverse:
            hsum = hout + hf_ref[:, hs]
            normed = _rms(hsum, ng_ref[:, hs])
            out_ref[:, hs] = (normed * mo_ref[:, hs].astype(F32)).astype(out_ref.dtype)
        else:
            out_ref[:, hs] = hout


def _mlstm(qk, proj, gates, geom, h_fwd=None, m_norm_g=None):
    T = qk.shape[0]
    reverse = h_fwd is not None
    nc = T // M_CHUNK
    cm = (lambda c: nc - 1 - c) if reverse else (lambda c: c)
    in_specs = [
        pl.BlockSpec((M_CHUNK, D_MODEL), lambda c: (cm(c), 0)),
        pl.BlockSpec((M_CHUNK, D_MODEL), lambda c: (cm(c), 1)),
        pl.BlockSpec((M_CHUNK, D_MODEL), lambda c: (cm(c), BLK_MV)),
        pl.BlockSpec((M_CHUNK, LANES), lambda c: (cm(c), 0)),
    ]
    args = [qk, qk, proj, gates]
    if reverse:
        in_specs += [
            pl.BlockSpec((M_CHUNK, D_MODEL), lambda c: (cm(c), 0)),
            pl.BlockSpec((M_CHUNK, D_MODEL), lambda c: (cm(c), BLK_MO)),
            pl.BlockSpec((1, D_MODEL), lambda c: (0, 0)),
        ]
        args += [h_fwd, proj, m_norm_g]
    return pl.pallas_call(
        functools.partial(_mlstm_kernel, reverse=reverse, geom=geom),
        grid=(nc,),
        in_specs=in_specs,
        out_specs=pl.BlockSpec((M_CHUNK, D_MODEL), lambda c: (cm(c), 0)),
        out_shape=jax.ShapeDtypeStruct((T, D_MODEL), BF16 if reverse else F32),
        scratch_shapes=[
            pltpu.VMEM((M_HEADS, M_HEAD_DIM, M_HEAD_DIM), F32),
            pltpu.VMEM((M_HEADS, 1, M_HEAD_DIM), F32),
            pltpu.VMEM((M_HEADS, 8, LANES), F32),
        ],
        compiler_params=_cparams(("arbitrary",)),
        name="mlstm_bwd" if reverse else "mlstm_fwd",
    )(*args)


ATT_RS = 16
A_VEXT = A_VDIM + 16


def _attn_kernel(q_ref, k_ref, v_ref, lam_ref, ng_ref, out_ref, s_scr, p_scr, acc_scr,
                 *, geom, tq, tk, kb, lam_init):
    tp, sp, ss = geom
    r0 = pl.program_id(1) * tq
    pos, slen = _seq_pos(r0, tp, sp, ss)
    off = (r0 - pos) % kb
    nk = slen // tk
    d = A_HEAD_DIM
    q = q_ref[...]
    lane = lax.broadcasted_iota(jnp.int32, q.shape, 1)
    zero = jnp.zeros_like(q)
    first = (lane // (d // 2)) % 2 == 0
    qs = (jnp.where(first, q, zero), jnp.where(first, zero, q))
    acc_scr[...] = jnp.zeros_like(acc_scr)
    nt = (((1,), (1,)), ((), ()))

    def scores(t, slot):
        start = pl.multiple_of(off + t * tk, tk)
        kc = k_ref[pl.ds(start, tk), :]
        mx = []
        for st in range(2):
            s_t = lax.dot_general(kc, qs[st], nt, preferred_element_type=F32)
            s_scr[slot, st] = s_t
            mx.append(jnp.max(s_t, axis=0, keepdims=True))
        return tuple(mx)

    def accumulate(t, slot, mx, ms):
        vt = v_ref[off // tk + t]
        new_m = []
        for st in range(2):
            m_new = jnp.maximum(ms[st], mx[st])
            alpha = jnp.exp2(ms[st] - m_new)
            m_blk = jnp.broadcast_to(m_new, (ATT_RS, tq))
            for kb in range(tk // ATT_RS):
                rows = slice(kb * ATT_RS, (kb + 1) * ATT_RS)
                p = jnp.exp2(s_scr[slot, st, rows, :] - m_blk)
                p_scr[st, rows, :] = p.astype(BF16)
            new_m.append(m_new)
            acc_scr[st] = alpha * acc_scr[st] + jnp.dot(vt, p_scr[st],
                                                        preferred_element_type=F32)
        return tuple(new_m)

    def body(u, carry):
        mx, ms = carry
        t = 2 * u
        mx_b = scores(t + 1, 1)
        ms = accumulate(t, 0, mx, ms)
        mx_a = scores(t + 2, 0)
        ms = accumulate(t + 1, 1, mx_b, ms)
        return mx_a, ms

    neg = jnp.full((1, tq), -jnp.inf, F32)
    mx, ms = lax.fori_loop(0, nk // 2 - 1, body, (scores(0, 0), (neg, neg)))
    mx_b = scores(nk - 1, 1)
    ms = accumulate(nk - 2, 0, mx, ms)
    accumulate(nk - 1, 1, mx_b, ms)

    lv = lam_ref[...]
    lam = (jnp.exp(jnp.sum(lv[0:1, :] * lv[1:2, :], axis=-1, keepdims=True))
           - jnp.exp(jnp.sum(lv[2:3, :] * lv[3:4, :], axis=-1, keepdims=True)) + lam_init)
    l1 = acc_scr[0, A_VDIM:A_VDIM + 1, :]
    l2 = acc_scr[1, A_VDIM:A_VDIM + 1, :]
    o_t = acc_scr[0, :A_VDIM, :] / l1 - lam * (acc_scr[1, :A_VDIM, :] / l2)
    inv = lax.rsqrt(jnp.mean(o_t * o_t, axis=0, keepdims=True) + EPS)
    y_t = o_t * inv * ng_ref[...] * (1.0 - lam_init)
    out_ref[...] = y_t.T.astype(out_ref.dtype)


def _attention(proj, lam_vecs, a_norm_g, geom, lam_init):
    T = proj.shape[0]
    tp, sp, ss = geom
    kb = max(sp, ss)
    tq = min(512, ss)
    tk = min(512, ss // 2)
    assert ss % (2 * tk) == 0 and sp % (2 * tk) == 0
    hpb = D_MODEL // A_VDIM
    v_t = proj[:, BLK_AV * D_MODEL:(BLK_AV + 1) * D_MODEL].reshape(T // tk, tk, A_HEADS, A_VDIM)
    v_t = v_t.transpose(0, 2, 3, 1)
    extra = jnp.zeros((T // tk, A_HEADS, A_VEXT - A_VDIM, tk), BF16).at[:, :, 0, :].set(1.0)
    v_t = jnp.concatenate([v_t, extra], axis=2)
    return pl.pallas_call(
        functools.partial(_attn_kernel, geom=geom, tq=tq, tk=tk, kb=kb, lam_init=lam_init),
        grid=(A_HEADS, T // tq),
        in_specs=[
            pl.BlockSpec((tq, A_VDIM), lambda h, r: (r, BLK_AQ * hpb + h)),
            pl.BlockSpec((kb, A_VDIM), lambda h, r: ((r * tq) // kb, BLK_AK * hpb + h)),
            pl.BlockSpec((kb // tk, None, A_VEXT, tk), lambda h, r: ((r * tq) // kb, h, 0, 0)),
            pl.BlockSpec((8, LANES), lambda h, r: (0, 0)),
            pl.BlockSpec((A_VDIM, 1), lambda h, r: (0, 0)),
        ],
        out_specs=pl.BlockSpec((tq, A_VDIM), lambda h, r: (r, h)),
        out_shape=jax.ShapeDtypeStruct((T, D_MODEL), BF16),
        scratch_shapes=[
            pltpu.VMEM((2, 2, tk, tq), F32),
            pltpu.VMEM((2, tk, tq), BF16),
            pltpu.VMEM((2, A_VEXT, tq), F32),
        ],
        compiler_params=_cparams(("parallel", "parallel")),
        name="diff_attn",
    )(proj, proj, v_t, lam_vecs, a_norm_g)


def _merge_kernel(x_ref, hm_ref, ha_ref, gm_ref, ga_ref, wm_ref, wa_ref, wo_ref, out_ref):
    bm = jnp.dot(hm_ref[...], wm_ref[...], preferred_element_type=F32)
    ba = jnp.dot(ha_ref[...], wa_ref[...], preferred_element_type=F32)
    merged = gm_ref[...].astype(F32) * bm + ga_ref[...].astype(F32) * ba
    out_ref[...] = x_ref[...] + jnp.dot(merged.astype(BF16), wo_ref[...],
                                        preferred_element_type=F32)


def _merge(x, hm, ha, proj, w_m, w_a, w_o):
    T = x.shape[0]
    tm = 512
    row = lambda i: (i, 0)
    const = lambda i: (0, 0)
    wspec = pl.BlockSpec((D_MODEL, D_MODEL), const)
    return pl.pallas_call(
        _merge_kernel,
        grid=(T // tm,),
        in_specs=[
            pl.BlockSpec((tm, D_MODEL), row),
            pl.BlockSpec((tm, D_MODEL), row),
            pl.BlockSpec((tm, D_MODEL), row),
            pl.BlockSpec((tm, D_MODEL), lambda i: (i, BLK_GM)),
            pl.BlockSpec((tm, D_MODEL), lambda i: (i, BLK_GA)),
            wspec, wspec, wspec,
        ],
        out_specs=pl.BlockSpec((tm, D_MODEL), row),
        out_shape=jax.ShapeDtypeStruct((T, D_MODEL), F32),
        compiler_params=_cparams(("parallel",)),
        name="merge_out",
    )(x, hm, ha, proj, proj, w_m, w_a, w_o)


FFN_SUB = 256


def _ffn_kernel(x_ref, g_ref, wg_ref, wu_ref, wd_ref, out_ref):
    x = x_ref[...]
    hb = _rms(x, g_ref[...]).astype(BF16)
    acc = x
    for s in range(F_PAD // FFN_SUB):
        cs = slice(s * FFN_SUB, (s + 1) * FFN_SUB)
        gate = jnp.dot(hb, wg_ref[:, cs], preferred_element_type=F32)
        up = jnp.dot(hb, wu_ref[:, cs], preferred_element_type=F32)
        hid = (gate * _sigmoid(gate) * up).astype(BF16)
        acc = acc + jnp.dot(hid, wd_ref[cs, :], preferred_element_type=F32)
    out_ref[...] = acc


def _ffn(x, g, wg, wu, wd):
    T = x.shape[0]
    tm = 512
    row = lambda i: (i, 0)
    const = lambda i: (0, 0)
    once = pl.Buffered(1)
    return pl.pallas_call(
        _ffn_kernel,
        grid=(T // tm,),
        in_specs=[
            pl.BlockSpec((tm, D_MODEL), row),
            pl.BlockSpec((1, D_MODEL), const),
            pl.BlockSpec((D_MODEL, F_PAD), const, pipeline_mode=once),
            pl.BlockSpec((D_MODEL, F_PAD), const, pipeline_mode=once),
            pl.BlockSpec((F_PAD, D_MODEL), const, pipeline_mode=once),
        ],
        out_specs=pl.BlockSpec((tm, D_MODEL), row),
        out_shape=jax.ShapeDtypeStruct((T, D_MODEL), F32),
        compiler_params=_cparams(("parallel",)),
        name="ffn_dense",
    )(x, g, wg, wu, wd)


MOE_TM = 1024
MOE_RB = 320
MOE_NF = 2
CUM_W = 256


def _router_kernel(x_ref, g_ref, rt_ref, post_ref, posc_ref, gatet_ref, cnt_ref):
    tm = x_ref.shape[0]
    h = _rms(x_ref[...], g_ref[...])
    rt = rt_ref[...]
    h_hi = h.astype(BF16)
    h_lo = (h - h_hi.astype(F32)).astype(BF16)
    r_hi = rt.astype(BF16)
    r_lo = (rt - r_hi.astype(F32)).astype(BF16)
    nt = (((1,), (1,)), ((), ()))
    logits = (lax.dot_general(r_hi, h_hi, nt, preferred_element_type=F32)
              + lax.dot_general(r_hi, h_lo, nt, preferred_element_type=F32)
              + lax.dot_general(r_lo, h_hi, nt, preferred_element_type=F32))
    eidx = lax.broadcasted_iota(jnp.int32, logits.shape, 0)
    v1 = jnp.max(logits, axis=0, keepdims=True)
    i1 = jnp.min(jnp.where(logits == v1, eidx, N_EXPERTS), axis=0, keepdims=True)
    sel1 = eidx == i1
    rest = jnp.where(sel1, -jnp.inf, logits)
    v2 = jnp.max(rest, axis=0, keepdims=True)
    i2 = jnp.min(jnp.where(rest == v2, eidx, N_EXPERTS), axis=0, keepdims=True)
    sel2 = eidx == i2
    e2 = jnp.exp(v2 - v1)
    p1 = 1.0 / (1.0 + e2)
    p2 = e2 / (1.0 + e2)
    gate_t = jnp.where(sel1, p1, 0.0) + jnp.where(sel2, p2, 0.0)
    sel = jnp.where(sel1 | sel2, 1.0, 0.0)
    r = lax.broadcasted_iota(jnp.int32, (CUM_W, CUM_W), 0)
    c = lax.broadcasted_iota(jnp.int32, (CUM_W, CUM_W), 1)
    upper = jnp.where(r <= c, 1.0, 0.0).astype(BF16)
    carry = jnp.zeros((N_EXPERTS, 1), F32)
    pieces = []
    for b in range(tm // CUM_W):
        blk = sel[:, b * CUM_W:(b + 1) * CUM_W].astype(BF16)
        cs = jnp.dot(blk, upper, preferred_element_type=F32) + carry
        pieces.append(cs)
        carry = cs[:, CUM_W - 1:CUM_W]
    incl = jnp.concatenate(pieces, axis=1)
    pos_t = jnp.where(sel > 0.0, incl - 1.0, -1.0)
    post_ref[...] = pos_t.astype(jnp.int32)
    gatet_ref[...] = gate_t
    cnt_ref[...] = jnp.broadcast_to(carry.astype(jnp.int32)[None], cnt_ref.shape)
    padded = jnp.concatenate([pos_t, jnp.full((LANES - N_EXPERTS, tm), -1.0, F32)], axis=0)
    posc_ref[...] = padded.T.astype(jnp.int32)


def _router(x, g, router_t):
    T = x.shape[0]
    nt = T // MOE_TM
    return pl.pallas_call(
        _router_kernel,
        grid=(nt,),
        in_specs=[
            pl.BlockSpec((MOE_TM, D_MODEL), lambda i: (i, 0)),
            pl.BlockSpec((1, D_MODEL), lambda i: (0, 0)),
            pl.BlockSpec((N_EXPERTS, D_MODEL), lambda i: (0, 0)),
        ],
        out_specs=[
            pl.BlockSpec((N_EXPERTS, MOE_TM), lambda i: (0, i)),
            pl.BlockSpec((MOE_TM, LANES), lambda i: (i, 0)),
            pl.BlockSpec((N_EXPERTS, MOE_TM), lambda i: (0, i)),
            pl.BlockSpec((1, N_EXPERTS, LANES), lambda i: (i, 0, 0)),
        ],
        out_shape=[
            jax.ShapeDtypeStruct((N_EXPERTS, T), jnp.int32),
            jax.ShapeDtypeStruct((T, LANES), jnp.int32),
            jax.ShapeDtypeStruct((N_EXPERTS, T), F32),
            jax.ShapeDtypeStruct((nt, N_EXPERTS, LANES), jnp.int32),
        ],
        compiler_params=_cparams(("parallel",)),
        name="moe_router",
    )(x, g, router_t)


def _moe_kernel(cnt_ref, x_ref, g_ref, post_ref, posc_ref, gatet_ref, wg_ref, wu_ref, wd_ref,
                out_ref, h_scr, xs_scr, ys_scr, gc_scr):
    i = pl.program_id(0)
    e = pl.program_id(1)
    f = pl.program_id(2)
    tm = x_ref.shape[0]
    nblk = (cnt_ref[i * N_EXPERTS + e] + MOE_RB - 1) // MOE_RB

    @pl.when((e == 0) & (f == 0))
    def _():
        x = x_ref[...]
        h_scr[...] = _rms(x, g_ref[...]).astype(BF16)
        out_ref[...] = x

    @pl.when(f == 0)
    def _():
        pos_row = post_ref[pl.ds(e, 1), :]
        gate_row = gatet_ref[pl.ds(e, 1), :]
        slot = lax.broadcasted_iota(jnp.int32, (MOE_RB, tm), 0)

        def gather(r, _):
            hit = pos_row == slot + r * MOE_RB
            onehot = jnp.where(hit, 1.0, 0.0).astype(BF16)
            xs_scr[r] = jnp.dot(onehot, h_scr[...],
                                preferred_element_type=F32).astype(BF16)
            gsel = jnp.sum(jnp.where(hit, gate_row, 0.0), axis=-1, keepdims=True)
            gc_scr[r] = jnp.broadcast_to(gsel, gc_scr.shape[1:])
            return 0

        lax.fori_loop(0, nblk, gather, 0)

    def expert(r, _):
        xs = xs_scr[r]
        gate = jnp.dot(xs, wg_ref[...], preferred_element_type=F32)
        up = jnp.dot(xs, wu_ref[...], preferred_element_type=F32)
        hid = (gate * _sigmoid(gate) * up).astype(BF16)
        y = jnp.dot(hid, wd_ref[...], preferred_element_type=F32)

        @pl.when(f == 0)
        def _():
            ys_scr[r] = y

        @pl.when(f != 0)
        def _():
            ys_scr[r] = ys_scr[r] + y

        return 0

    lax.fori_loop(0, nblk, expert, 0)

    @pl.when(f == MOE_NF - 1)
    def _():
        lane = lax.broadcasted_iota(jnp.int32, posc_ref.shape, 1)
        pos_col = jnp.sum(jnp.where(lane == e, posc_ref[...], 0), axis=-1, keepdims=True)
        slot = lax.broadcasted_iota(jnp.int32, (tm, MOE_RB), 1)

        def scatter(r, _):
            onehot_t = jnp.where(pos_col == slot + r * MOE_RB, 1.0, 0.0).astype(BF16)
            ysg = (ys_scr[r] * gc_scr[r][:, 0:1]).astype(BF16)
            out_ref[...] += jnp.dot(onehot_t, ysg, preferred_element_type=F32)
            return 0

        lax.fori_loop(0, nblk, scatter, 0)


def _moe(x, g, pos_t, pos_c, gate_t, counts, wg, wu, wd):
    T = x.shape[0]
    nt = T // MOE_TM
    tf = F_PAD // MOE_NF
    nrb = -(-MOE_TM // MOE_RB)
    grid_spec = pltpu.PrefetchScalarGridSpec(
        num_scalar_prefetch=1,
        grid=(nt, N_EXPERTS, MOE_NF),
        in_specs=[
            pl.BlockSpec((MOE_TM, D_MODEL), lambda i, e, f, c: (i, 0)),
            pl.BlockSpec((1, D_MODEL), lambda i, e, f, c: (0, 0)),
            pl.BlockSpec((N_EXPERTS, MOE_TM), lambda i, e, f, c: (0, i)),
            pl.BlockSpec((MOE_TM, LANES), lambda i, e, f, c: (i, 0)),
            pl.BlockSpec((N_EXPERTS, MOE_TM), lambda i, e, f, c: (0, i)),
            pl.BlockSpec((None, D_MODEL, tf), lambda i, e, f, c: (e, 0, f)),
            pl.BlockSpec((None, D_MODEL, tf), lambda i, e, f, c: (e, 0, f)),
            pl.BlockSpec((None, tf, D_MODEL), lambda i, e, f, c: (e, f, 0)),
        ],
        out_specs=pl.BlockSpec((MOE_TM, D_MODEL), lambda i, e, f, c: (i, 0)),
        scratch_shapes=[
            pltpu.VMEM((MOE_TM, D_MODEL), BF16),
            pltpu.VMEM((nrb, MOE_RB, D_MODEL), BF16),
            pltpu.VMEM((nrb, MOE_RB, D_MODEL), F32),
            pltpu.VMEM((nrb, MOE_RB, LANES), F32),
        ],
    )
    return pl.pallas_call(
        _moe_kernel,
        grid_spec=grid_spec,
        out_shape=jax.ShapeDtypeStruct((T, D_MODEL), F32),
        compiler_params=_cparams(("parallel", "arbitrary", "arbitrary")),
        name="moe_experts",
    )(counts, x, g, pos_t, pos_c, gate_t, wg, wu, wd)


def _norm_kernel(x_ref, g_ref, out_ref):
    out_ref[...] = _rms(x_ref[...], g_ref[...])


def _final_norm(x, g):
    T = x.shape[0]
    tm = 1024
    return pl.pallas_call(
        _norm_kernel,
        grid=(T // tm,),
        in_specs=[pl.BlockSpec((tm, D_MODEL), lambda i: (i, 0)),
                  pl.BlockSpec((1, D_MODEL), lambda i: (0, 0))],
        out_specs=pl.BlockSpec((tm, D_MODEL), lambda i: (i, 0)),
        out_shape=jax.ShapeDtypeStruct((T, D_MODEL), F32),
        compiler_params=_cparams(("parallel",)),
        name="final_norm",
    )(x, g)


def _rope_tables(smax):
    d = A_HEAD_DIM
    inv = 1.0 / (ROPE_THETA ** (jnp.arange(0, d, 2, dtype=F32) / d))
    ang = jnp.arange(smax, dtype=F32)[:, None] * inv[None, :]
    cos, sin = jnp.cos(ang), jnp.sin(ang)
    cos_t = jnp.concatenate([cos, cos, cos, cos], axis=-1)
    sin_t = jnp.concatenate([-sin, -sin, sin, sin], axis=-1)
    return cos_t, sin_t


def _rope_perm():
    half = A_HEAD_DIM // 2
    one = [0 * half, 2 * half, 1 * half, 3 * half]
    perm = [s + i for s in one for i in range(half)]
    return jnp.asarray([g * LANES + p for g in range(D_MODEL // LANES) for p in perm],
                       dtype=jnp.int32)


def _split_gu(w_gu):
    pad = [(0, 0)] * (w_gu.ndim - 1) + [(0, F_PAD - D_FF)]
    wg = jnp.pad(w_gu[..., :D_FF], pad).astype(BF16)
    wu = jnp.pad(w_gu[..., D_FF:], pad).astype(BF16)
    return wg, wu


def _pad_down(w_down):
    pad = [(0, 0)] * (w_down.ndim - 2) + [(0, F_PAD - D_FF), (0, 0)]
    return jnp.pad(w_down, pad).astype(BF16)


def kernel(x_prompt, x_sample, norm_mix_g, w_in, conv_qk, b_igate, b_fgate, m_norm_g, lam_q1, lam_k1, lam_q2, lam_k2, a_norm_g, w_br_m, w_br_a, w_out, norm_ffn_g, dense_w_gu, dense_w_down, moe_router, moe_w_gu, moe_w_down, final_norm_g):
    bp, sp, _ = x_prompt.shape
    bs, ss, _ = x_sample.shape
    tp, ts = bp * sp, bs * ss
    kb = max(sp, ss)
    assert sp % ss == 0 and tp % kb == 0 and ts % kb == 0 and ss % M_CHUNK == 0
    assert tp % MOE_TM == 0 and ts % MOE_TM == 0
    geom = (tp, sp, ss)
    depth = w_in.shape[0]
    n_gate = 4 * M_HEADS
    g0 = 4 * D_MODEL

    x = jnp.concatenate([x_prompt.reshape(tp, D_MODEL), x_sample.reshape(ts, D_MODEL)], axis=0)
    cos_t, sin_t = _rope_tables(kb)
    perm = _rope_perm()
    row = lambda v: v.reshape(1, -1).astype(F32)

    for l in range(depth):
        lam_init = 0.8 - 0.6 * math.exp(-0.3 * l)
        w_rest = w_in[l][:, g0 + n_gate:]
        w_main = jnp.concatenate(
            [w_in[l][:, :g0], w_rest[:, :D_MODEL][:, perm], w_rest[:, D_MODEL:2 * D_MODEL][:, perm],
             w_rest[:, 2 * D_MODEL:]], axis=1).astype(BF16)
        w_gate = jnp.pad(w_in[l][:, g0:g0 + n_gate], ((0, 0), (0, LANES - n_gate))).astype(BF16)
        b_gate = jnp.pad(jnp.concatenate([b_igate[l].reshape(-1), b_fgate[l].reshape(-1)]),
                         (0, LANES - n_gate)).reshape(1, LANES).astype(F32)
        proj, gates = _inproj(x, row(norm_mix_g[l]), w_main, w_gate, b_gate, cos_t, sin_t, geom)

        qk = _conv_silu(proj, conv_qk[l].astype(F32), geom)
        h_fwd = _mlstm(qk, proj, gates, geom)
        hm = _mlstm(qk, proj, gates, geom, h_fwd=h_fwd, m_norm_g=row(m_norm_g[l]))

        lam_vecs = jnp.pad(jnp.stack([lam_q1[l], lam_k1[l], lam_q2[l], lam_k2[l]]).astype(F32),
                           ((0, 4), (0, LANES - A_HEAD_DIM)))
        ha = _attention(proj, lam_vecs, a_norm_g[l].reshape(A_VDIM, 1).astype(F32), geom,
                        lam_init)

        x = _merge(x, hm, ha, proj, w_br_m[l].astype(BF16), w_br_a[l].astype(BF16),
                   w_out[l].astype(BF16))

        if l % 2 == 0:
            wg, wu = _split_gu(dense_w_gu[l // 2])
            x = _ffn(x, row(norm_ffn_g[l]), wg, wu, _pad_down(dense_w_down[l // 2]))
        else:
            g = row(norm_ffn_g[l])
            pos_t, pos_c, gate_t, cnt = _router(x, g, moe_router[l // 2].T.astype(F32))
            wg, wu = _split_gu(moe_w_gu[l // 2])
            x = _moe(x, g, pos_t, pos_c, gate_t, cnt[:, :, 0].reshape(-1), wg, wu,
                     _pad_down(moe_w_down[l // 2]))

    y = _final_norm(x, row(final_norm_g))
    return (y[:tp].reshape(bp, sp, D_MODEL), y[tp:].reshape(bs, ss, D_MODEL))
```
